```python
import jax, jax.numpy as jnp
from jax import lax
import numpy as np

D_MODEL = 2048
BATCH = 1
SEQ = 16384
DEPTH = 1
DEC_BATCH = 32
DEC_SEQ = 4
PAST_LEN = 16384
PAGE_SIZE = 128

MIX_W = D_MODEL // 2
SB_HEADS = 8
SB_HEAD_DIM = MIX_W // SB_HEADS
SB_BIAS_INIT = 9.0
SB_K_SCALE = 0.1
RG_W = MIX_W
RG_BLOCKS = 8
RG_BLOCK_W = RG_W // RG_BLOCKS
RG_C = 8.0
CONV_W = 4
MEM_TOKENS = 256
MEM_HEADS = 4
MEM_HEAD_DIM = MIX_W // MEM_HEADS
N_BRANCH = 3
D_FF = 256 * ((8 * D_MODEL // 3 + 255) // 256)
Q_BLOCK = 128
EPS = 1e-6
IN_SIZES = (MIX_W, MIX_W, MIX_W, RG_W, RG_W, MIX_W, N_BRANCH * D_MODEL)

kernel_name = "hybrid_stickbreak_rglru_memxattn_step"


def rms_norm(x, g):
    xf = x.astype(jnp.float32)
    y = xf * lax.rsqrt(jnp.mean(xf * xf, axis=-1, keepdims=True) + EPS)
    return (y * g.astype(jnp.float32)).astype(x.dtype)


def macaron_ffn(x, g_pre, g_post, w_gu, w_down):
    h = rms_norm(x, g_pre)
    gate, up = jnp.split(h @ w_gu, 2, axis=-1)
    y = (jax.nn.silu(gate) * up) @ w_down
    return x + 0.5 * rms_norm(y, g_post)


def stick_breaking_core(q, k, v, q_pos, k_pos, bias):
    z = jnp.einsum('bqhd,bkhd->bhqk', q, k, preferred_element_type=jnp.float32) * (SB_HEAD_DIM ** -0.5)
    z = z + bias.astype(jnp.float32)[None, :, None, None]
    mask = k_pos[None, :] < q_pos[:, None]
    log_keep = jnp.where(mask, jax.nn.log_sigmoid(-z), 0.0)
    incl = lax.cumsum(log_keep, axis=3, reverse=True)
    excl = jnp.concatenate([incl[..., 1:], jnp.zeros_like(incl[..., :1])], axis=-1)
    w = jnp.where(mask, jnp.exp(jax.nn.log_sigmoid(z) + excl), 0.0)
    return jnp.einsum('bhqk,bkhd->bqhd', w.astype(v.dtype), v)


def stick_breaking(q, k, v, q_pos, k_pos, bias):
    B, T, H, Dh = q.shape
    if T > Q_BLOCK and T % Q_BLOCK == 0:
        nb = T // Q_BLOCK
        qb = q.reshape(B, nb, Q_BLOCK, H, Dh).transpose(1, 0, 2, 3, 4)
        pb = q_pos.reshape(nb, Q_BLOCK)
        ob = lax.map(lambda a: stick_breaking_core(a[0], k, v, a[1], k_pos, bias), (qb, pb))
        return ob.transpose(1, 0, 2, 3, 4).reshape(B, T, H, Dh)
    return stick_breaking_core(q, k, v, q_pos, k_pos, bias)


def causal_conv(x, buf, w, b):
    T = x.shape[1]
    xp = jnp.concatenate([buf.astype(x.dtype), x], axis=1)
    y = b + xp[:, 0:T] * w[0]
    for tap in range(1, CONV_W):
        y = y + xp[:, tap:tap + T] * w[tap]
    return y, xp[:, T:]


def rg_lru(x, h0, pos, w_a, b_a, w_x, b_x, lam):
    B, T, W = x.shape
    xb = x.reshape(B, T, RG_BLOCKS, RG_BLOCK_W)
    r = jax.nn.sigmoid((jnp.einsum('btni,nij->btnj', xb, w_a).reshape(B, T, W) + b_a).astype(jnp.float32))
    i = jax.nn.sigmoid((jnp.einsum('btni,nij->btnj', xb, w_x).reshape(B, T, W) + b_x).astype(jnp.float32))
    log_a = -RG_C * r * jax.nn.softplus(-lam.astype(jnp.float32))
    reset = (pos == 0)[None, :, None]
    a = jnp.where(reset, 0.0, jnp.exp(log_a))
    mult = jnp.where(reset, 1.0, jnp.sqrt(-jnp.expm1(2.0 * log_a)))
    bterm = mult * i * x.astype(jnp.float32)
    bterm = bterm.at[:, 0].add(a[:, 0] * h0.astype(jnp.float32))

    def combine(left, right):
        a1, b1 = left
        a2, b2 = right
        return a1 * a2, a2 * b1 + b2

    _, h = lax.associative_scan(combine, (a, bterm), axis=1)
    return h.astype(x.dtype), h[:, -1].astype(h0.dtype)


def mem_kv(mem, g, w_kv):
    B, M, _ = mem.shape
    k, v = jnp.split(rms_norm(mem, g) @ w_kv, 2, axis=-1)
    return (k.reshape(B, M, MEM_HEADS, MEM_HEAD_DIM), v.reshape(B, M, MEM_HEADS, MEM_HEAD_DIM))


def mem_attention(q, mk, mv):
    s = jnp.einsum('bthd,bmhd->bhtm', q, mk, preferred_element_type=jnp.float32) * (MEM_HEAD_DIM ** -0.5)
    p = jax.nn.softmax(s, axis=-1)
    return jnp.einsum('bhtm,bmhd->bthd', p.astype(mv.dtype), mv)


def decoder_layer(x, pos, k_past, v_past, conv_buf, h0, mem_k, mem_v, p):
    B, T, _ = x.shape
    x = macaron_ffn(x, p['ffn1_g_pre'], p['ffn1_g_post'], p['ffn1_w_gu'], p['ffn1_w_down'])
    h = rms_norm(x, p['mix_g_pre'])
    splits = [int(s) for s in np.cumsum(IN_SIZES)[:-1]]
    q, k, v, xr, gr, qm, gl = jnp.split(h @ p['w_in'], splits, axis=-1)
    q = q.reshape(B, T, SB_HEADS, SB_HEAD_DIM)
    k = k.reshape(B, T, SB_HEADS, SB_HEAD_DIM)
    v = v.reshape(B, T, SB_HEADS, SB_HEAD_DIM)
    if k_past is None:
        k_all, v_all, k_pos = k, v, pos
    else:
        past_len = k_past.shape[1]
        k_all = jnp.concatenate([k_past.astype(k.dtype), k], axis=1)
        v_all = jnp.concatenate([v_past.astype(v.dtype), v], axis=1)
        k_pos = jnp.concatenate([jnp.arange(past_len, dtype=jnp.int32), pos])
    o_sb = stick_breaking(q, k_all, v_all, pos, k_pos, p['sb_bias'])
    o_sb = rms_norm(o_sb, p['sb_norm_g']).reshape(B, T, MIX_W)
    xc, conv_new = causal_conv(xr, conv_buf, p['conv_w'], p['conv_b'])
    hr, h_new = rg_lru(xc, h0, pos, p['rg_w_a'], p['rg_b_a'], p['rg_w_x'], p['rg_b_x'], p['rg_lambda'])
    o_rg = jax.nn.gelu(gr) * hr
    o_mem = mem_attention(qm.reshape(B, T, MEM_HEADS, MEM_HEAD_DIM), mem_k.astype(qm.dtype), mem_v.astype(qm.dtype)).reshape(B, T, MIX_W)
    gates = jax.nn.sigmoid(gl + p['b_gate']).reshape(B, T, N_BRANCH, D_MODEL)
    branches = (o_sb, o_rg, o_mem)
    merged = gates[:, :, 0] * (branches[0] @ p['w_branch'][0])
    for n in range(1, N_BRANCH):
        merged = merged + gates[:, :, n] * (branches[n] @ p['w_branch'][n])
    x = x + rms_norm(merged @ p['w_out'], p['mix_g_post'])
    x = macaron_ffn(x, p['ffn2_g_pre'], p['ffn2_g_post'], p['ffn2_w_gu'], p['ffn2_w_down'])
    return x, k, v, conv_new, h_new


def setup_inputs(seed: int = 0) -> dict:
    key = jax.random.key(seed)
    ks = jax.random.split(key, 40)
    nrm = jax.random.normal
    n_pages = PAST_LEN // PAGE_SIZE
    n_used = DEC_BATCH * n_pages
    n_pool = n_used + n_used // 4
    page_table = jax.random.permutation(ks[0], n_pool)[:n_used].reshape(DEC_BATCH, n_pages).astype(jnp.int32)

    def gain(k, n=D_MODEL):
        return 1.0 + 0.05 * nrm(k, (n,), jnp.float32)

    u = jax.random.uniform(ks[1], (RG_W,), jnp.float32, minval=0.9, maxval=0.999)
    a0 = u ** (1.0 / RG_C)
    rg_lambda = jnp.log(a0) - jnp.log1p(-a0)
    total_in = sum(IN_SIZES)
    col_scale = jnp.ones((total_in,), jnp.float32).at[MIX_W:2 * MIX_W].set(SB_K_SCALE)
    return {
        'x_prompt': nrm(ks[2], (BATCH, SEQ, D_MODEL), jnp.float32),
        'x_sample': nrm(ks[3], (DEC_BATCH, DEC_SEQ, D_MODEL), jnp.float32),
        'cache_sb_k': SB_K_SCALE * nrm(ks[4], (n_pool, PAGE_SIZE, SB_HEADS, SB_HEAD_DIM), jnp.float32),
        'cache_sb_v': nrm(ks[5], (n_pool, PAGE_SIZE, SB_HEADS, SB_HEAD_DIM), jnp.float32),
        'state_conv': nrm(ks[6], (DEC_BATCH, CONV_W - 1, RG_W), jnp.float32),
        'state_rglru': 0.5 * nrm(ks[7], (DEC_BATCH, RG_W), jnp.float32),
        'cache_mem_k': nrm(ks[8], (DEC_BATCH, MEM_TOKENS, MEM_HEADS, MEM_HEAD_DIM), jnp.float32),
        'cache_mem_v': nrm(ks[9], (DEC_BATCH, MEM_TOKENS, MEM_HEADS, MEM_HEAD_DIM), jnp.float32),
        'page_table': page_table,
        'mem_prompt': nrm(ks[10], (BATCH, MEM_TOKENS, D_MODEL), jnp.float32),
        'ffn1_g_pre': gain(ks[11]),
        'ffn1_g_post': gain(ks[12]),
        'ffn1_w_gu': nrm(ks[13], (D_MODEL, 2 * D_FF), jnp.float32) * D_MODEL ** -0.5,
        'ffn1_w_down': nrm(ks[14], (D_FF, D_MODEL), jnp.float32) * D_FF ** -0.5,
        'mix_g_pre': gain(ks[15]),
        'mix_g_post': gain(ks[16]),
        'w_in': nrm(ks[17], (D_MODEL, total_in), jnp.float32) * D_MODEL ** -0.5 * col_scale,
        'b_gate': 0.01 * nrm(ks[18], (N_BRANCH * D_MODEL,), jnp.float32),
        'sb_bias': -SB_BIAS_INIT + 0.1 * nrm(ks[33], (SB_HEADS,), jnp.float32),
        'sb_norm_g': gain(ks[34], SB_HEAD_DIM),
        'conv_w': nrm(ks[19], (CONV_W, RG_W), jnp.float32) * CONV_W ** -0.5,
        'conv_b': 0.01 * nrm(ks[20], (RG_W,), jnp.float32),
        'rg_w_a': nrm(ks[21], (RG_BLOCKS, RG_BLOCK_W, RG_BLOCK_W), jnp.float32) * RG_BLOCK_W ** -0.5,
        'rg_b_a': 0.01 * nrm(ks[22], (RG_W,), jnp.float32),
        'rg_w_x': nrm(ks[23], (RG_BLOCKS, RG_BLOCK_W, RG_BLOCK_W), jnp.float32) * RG_BLOCK_W ** -0.5,
        'rg_b_x': 0.01 * nrm(ks[24], (RG_W,), jnp.float32),
        'rg_lambda': rg_lambda,
        'mem_g': gain(ks[25]),
        'w_mem_kv': nrm(ks[26], (D_MODEL, 2 * MIX_W), jnp.float32) * D_MODEL ** -0.5,
        'w_branch': nrm(ks[27], (N_BRANCH, MIX_W, D_MODEL), jnp.float32) * MIX_W ** -0.5,
        'w_out': nrm(ks[28], (D_MODEL, D_MODEL), jnp.float32) * D_MODEL ** -0.5,
        'ffn2_g_pre': gain(ks[29]),
        'ffn2_g_post': gain(ks[30]),
        'ffn2_w_gu': nrm(ks[31], (D_MODEL, 2 * D_FF), jnp.float32) * D_MODEL ** -0.5,
        'ffn2_w_down': nrm(ks[32], (D_FF, D_MODEL), jnp.float32) * D_FF ** -0.5,
    }


def reference(x_prompt, x_sample, cache_sb_k, cache_sb_v, state_conv, state_rglru, cache_mem_k, cache_mem_v, page_table, mem_prompt, ffn1_g_pre, ffn1_g_post, ffn1_w_gu, ffn1_w_down, mix_g_pre, mix_g_post, w_in, b_gate, sb_bias, sb_norm_g, conv_w, conv_b, rg_w_a, rg_b_a, rg_w_x, rg_b_x, rg_lambda, mem_g, w_mem_kv, w_branch, w_out, ffn2_g_pre, ffn2_g_post, ffn2_w_gu, ffn2_w_down):
    p = dict(ffn1_g_pre=ffn1_g_pre, ffn1_g_post=ffn1_g_post, ffn1_w_gu=ffn1_w_gu, ffn1_w_down=ffn1_w_down,
             mix_g_pre=mix_g_pre, mix_g_post=mix_g_post, w_in=w_in, b_gate=b_gate,
             sb_bias=sb_bias, sb_norm_g=sb_norm_g,
             conv_w=conv_w, conv_b=conv_b, rg_w_a=rg_w_a, rg_b_a=rg_b_a, rg_w_x=rg_w_x, rg_b_x=rg_b_x,
             rg_lambda=rg_lambda, w_branch=w_branch, w_out=w_out,
             ffn2_g_pre=ffn2_g_pre, ffn2_g_post=ffn2_g_post, ffn2_w_gu=ffn2_w_gu, ffn2_w_down=ffn2_w_down)
    Bp, Tp, _ = x_prompt.shape
    Bs, Ts, _ = x_sample.shape
    n_pages = page_table.shape[1]
    past_len = n_pages * cache_sb_k.shape[1]
    pos_p = jnp.arange(Tp, dtype=jnp.int32)
    pos_s = past_len + jnp.arange(Ts, dtype=jnp.int32)
    k_past = cache_sb_k[page_table].reshape(Bs, past_len, SB_HEADS, SB_HEAD_DIM)
    v_past = cache_sb_v[page_table].reshape(Bs, past_len, SB_HEADS, SB_HEAD_DIM)
    mem_k_p, mem_v_p = mem_kv(mem_prompt, mem_g, w_mem_kv)
    conv0 = jnp.zeros((Bp, CONV_W - 1, RG_W), x_prompt.dtype)
    h0 = jnp.zeros((Bp, RG_W), x_prompt.dtype)
    y_p, y_s = x_prompt, x_sample
    for _ in range(DEPTH):
        y_p, k_p, v_p, conv_p, h_p = decoder_layer(y_p, pos_p, None, None, conv0, h0, mem_k_p, mem_v_p, p)
        y_s, k_s, v_s, conv_s, h_s = decoder_layer(y_s, pos_s, k_past, v_past, state_conv, state_rglru, cache_mem_k, cache_mem_v, p)
    return (y_p, y_s, k_p, v_p, k_s, v_s, conv_p, conv_s, h_p, h_s, mem_k_p, mem_v_p)
```

```python
import functools
import math

import jax
import jax.numpy as jnp
from jax import lax
from jax.experimental import pallas as pl
from jax.experimental.pallas import tpu as pltpu

F32 = jnp.float32
BF16 = jnp.bfloat16

EPS = 1e-6
RG_C = 8.0
N_BRANCH = 3
SB_HEADS = 8
MEM_HEADS = 4
RG_BLOCKS = 8

V7X_LANES = 128
V7X_SUBLANES = 8
V7X_MXU_DIM = 256
V7X_VMEM_BYTES = 64 * 1024 * 1024
VMEM_LIMIT = V7X_VMEM_BYTES - 8 * 1024 * 1024

_NT = (((1,), (1,)), ((), ()))
_TN = (((0,), (0,)), ((), ()))


def _params(*sem):
    return pltpu.CompilerParams(dimension_semantics=sem, vmem_limit_bytes=VMEM_LIMIT)


def _rms(x, g):
    ms = jnp.mean(x * x, axis=-1, keepdims=True)
    return x * lax.rsqrt(ms + EPS) * g


def _softplus(z):
    return jnp.maximum(z, 0.0) + jnp.log(1.0 + jnp.exp(-jnp.abs(z)))


def _split_bf16(x):
    hi = x.astype(BF16)
    lo = (x - hi.astype(F32)).astype(BF16)
    return hi, lo


def _row_block(m, target):
    return target if m % target == 0 else m


def _ffn_kernel(x_ref, gpre_ref, gpost_ref, wg_ref, wu_ref, wd_ref, o_ref, h_ref):
    j = pl.program_id(1)

    @pl.when(j == 0)
    def _():
        h_ref[...] = _rms(x_ref[...], gpre_ref[...]).astype(BF16)
        o_ref[...] = jnp.zeros_like(o_ref)

    h = h_ref[...]
    gate = jnp.dot(h, wg_ref[...], preferred_element_type=F32)
    up = jnp.dot(h, wu_ref[...], preferred_element_type=F32)
    act = (gate * jax.nn.sigmoid(gate) * up).astype(BF16)
    o_ref[...] += jnp.dot(act, wd_ref[...], preferred_element_type=F32)

    @pl.when(j == pl.num_programs(1) - 1)
    def _():
        o_ref[...] = x_ref[...] + 0.5 * _rms(o_ref[...], gpost_ref[...])


def _ffn(x, g_pre, g_post, w_gu, w_down, *, tm=512, tf=512):
    m, d = x.shape
    f = w_down.shape[0]
    tm = _row_block(m, tm)
    nf = f // tf
    return pl.pallas_call(
        _ffn_kernel,
        grid=(m // tm, nf),
        in_specs=[
            pl.BlockSpec((tm, d), lambda i, j: (i, 0)),
            pl.BlockSpec((1, d), lambda i, j: (0, 0)),
            pl.BlockSpec((1, d), lambda i, j: (0, 0)),
            pl.BlockSpec((d, tf), lambda i, j: (0, j)),
            pl.BlockSpec((d, tf), lambda i, j: (0, j + nf)),
            pl.BlockSpec((tf, d), lambda i, j: (j, 0)),
        ],
        out_specs=pl.BlockSpec((tm, d), lambda i, j: (i, 0)),
        out_shape=jax.ShapeDtypeStruct((m, d), F32),
        scratch_shapes=[pltpu.VMEM((tm, d), BF16)],
        compiler_params=_params("parallel", "arbitrary"),
        name="ffn",
    )(x, g_pre, g_post, w_gu, w_gu, w_down)


def _norm_proj_kernel(x_ref, g_ref, w_ref, o_ref, *rest, n_bf16):
    if n_bf16:
        obf_ref, h_ref = rest
    else:
        (h_ref,) = rest
    j = pl.program_id(1)

    @pl.when(j == 0)
    def _():
        h_ref[...] = _rms(x_ref[...], g_ref[...]).astype(BF16)

    y = jnp.dot(h_ref[...], w_ref[...], preferred_element_type=F32)
    o_ref[...] = y
    if n_bf16:
        @pl.when(j < n_bf16)
        def _():
            obf_ref[...] = y.astype(BF16)


def _norm_proj(x, g, w, *, tn, n_bf16=0, tm=1024):
    m, d = x.shape
    n = w.shape[1]
    tm = _row_block(m, tm)
    parts = n // tn
    out_shape = [jax.ShapeDtypeStruct((parts, m, tn), F32)]
    out_specs = [pl.BlockSpec((None, tm, tn), lambda i, j: (j, i, 0))]
    if n_bf16:
        out_shape.append(jax.ShapeDtypeStruct((n_bf16, m, tn), BF16))
        out_specs.append(
            pl.BlockSpec((None, tm, tn), lambda i, j: (jnp.minimum(j, n_bf16 - 1), i, 0)))
    res = pl.pallas_call(
        functools.partial(_norm_proj_kernel, n_bf16=n_bf16),
        grid=(m // tm, parts),
        in_specs=[
            pl.BlockSpec((tm, d), lambda i, j: (i, 0)),
            pl.BlockSpec((1, d), lambda i, j: (0, 0)),
            pl.BlockSpec((d, tn), lambda i, j: (0, j)),
        ],
        out_specs=out_specs,
        out_shape=out_shape,
        scratch_shapes=[pltpu.VMEM((tm, d), BF16)],
        compiler_params=_params("parallel", "arbitrary"),
        name="norm_proj",
    )(x, g, w)
    return res if n_bf16 else res[0]


def _sb_prompt_kernel(bias_ref, q_ref, k_ref, v_ref, u_ref, g_ref, o_ref, acc_ref, c_ref,
                      *, bq, bk, scale):
    h = pl.program_id(0)
    i = pl.program_id(1)
    bias = bias_ref[h]
    q = q_ref[...]
    u2 = u_ref[...]
    acc_ref[...] = jnp.zeros_like(acc_ref)
    c_ref[...] = jnp.zeros_like(c_ref)

    def block(kb, masked):
        start = pl.multiple_of(kb * bk, bk)
        kblk = k_ref[pl.ds(start, bk), :]
        vblk = v_ref[pl.ds(start, bk), :]
        z = lax.dot_general(q, kblk, _NT, preferred_element_type=F32) * scale + bias
        sp = _softplus(z)
        if masked:
            row = i * bq + lax.broadcasted_iota(jnp.int32, (bq, bk), 0)
            col = kb * bk + lax.broadcasted_iota(jnp.int32, (bq, bk), 1)
            keep = col < row
            sp = jnp.where(keep, sp, 0.0)
        hi, lo = _split_bf16(sp)
        excl = jnp.dot(jnp.concatenate([hi, lo], axis=1), u2, preferred_element_type=F32)
        c = c_ref[...]
        w = jnp.exp(z - sp - excl - c)
        if masked:
            w = jnp.where(keep, w, 0.0)
        acc_ref[...] += jnp.dot(w.astype(BF16), vblk, preferred_element_type=F32)
        c_ref[...] = c + jnp.sum(sp, axis=1, keepdims=True)

    r = bq // bk
    for d in range(r):
        block(i * r + (r - 1 - d), True)

    def body(n, carry):
        block(i * r - 1 - n, False)
        return carry

    lax.fori_loop(0, i * r, body, 0)
    o_ref[...] = _rms(acc_ref[...], g_ref[...]).astype(BF16)


def _sb_prompt(qkv, bias, u2, g, *, heads, bq=512, bk=V7X_MXU_DIM):
    _, t, w = qkv.shape
    dh = w // heads
    bq = bq if t % bq == 0 else bk
    assert t % bq == 0 and bq % bk == 0
    return pl.pallas_call(
        functools.partial(_sb_prompt_kernel, bq=bq, bk=bk, scale=dh ** -0.5),
        grid=(heads, t // bq),
        in_specs=[
            pl.BlockSpec(memory_space=pltpu.SMEM),
            pl.BlockSpec((None, bq, dh), lambda h, i: (0, i, h)),
            pl.BlockSpec((None, t, dh), lambda h, i: (1, 0, h)),
            pl.BlockSpec((None, t, dh), lambda h, i: (2, 0, h)),
            pl.BlockSpec((2 * bk, bk), lambda h, i: (0, 0)),
            pl.BlockSpec((1, dh), lambda h, i: (0, 0)),
        ],
        out_specs=pl.BlockSpec((bq, dh), lambda h, i: (i, h)),
        out_shape=jax.ShapeDtypeStruct((t, w), BF16),
        scratch_shapes=[pltpu.VMEM((bq, dh), F32), pltpu.VMEM((bq, 1), F32)],
        compiler_params=_params("parallel", "arbitrary"),
        name="sb_prompt",
    )(bias, qkv, qkv, qkv, u2, g)


_QPAD = 8


def _sb_sample_kernel(pt_ref, q_ref, kn_ref, vn_ref, bias_ref, ut_ref, g_ref, *rest,
                      pages_per_step, heads, scale):
    kv_refs = rest[:2 * pages_per_step]
    o_ref, qrows_ref, acc_ref, c_ref, kpad_ref, vpad_ref = rest[2 * pages_per_step:]
    del pt_ref
    s = pl.program_id(1)
    ts, w = q_ref.shape
    dh = w // heads
    ncol = heads * _QPAD
    page = kpad_ref.shape[0]

    def process(kf, vf, keep):
        kb = kf.astype(BF16)
        vb = vf.astype(BF16)
        z = lax.dot_general(kb, qrows_ref[...], _NT, preferred_element_type=F32)
        z = z * scale + bias_ref[...]
        sp = _softplus(z)
        if keep is not None:
            sp = jnp.where(keep, sp, 0.0)
        hi, lo = _split_bf16(sp)
        excl = jnp.dot(ut_ref[...], jnp.concatenate([hi, lo], axis=0),
                       preferred_element_type=F32)
        c = c_ref[...]
        wgt = jnp.exp(z - sp - excl - c)
        if keep is not None:
            wgt = jnp.where(keep, wgt, 0.0)
        acc_ref[...] += lax.dot_general(wgt.astype(BF16), vb, _TN, preferred_element_type=F32)
        c_ref[...] = c + jnp.sum(sp, axis=0, keepdims=True)

    @pl.when(s == 0)
    def _():
        q8 = jnp.concatenate([q_ref[...], jnp.zeros((_QPAD - ts, w), F32)], axis=0)
        qt = jnp.concatenate([q8] * heads, axis=0)
        row_head = lax.broadcasted_iota(jnp.int32, (ncol, w), 0) // _QPAD
        col_head = lax.broadcasted_iota(jnp.int32, (ncol, w), 1) // dh
        qrows_ref[...] = jnp.where(row_head == col_head, qt, 0.0).astype(BF16)
        acc_ref[...] = jnp.zeros_like(acc_ref)
        c_ref[...] = jnp.zeros_like(c_ref)
        kpad_ref[...] = jnp.zeros_like(kpad_ref)
        vpad_ref[...] = jnp.zeros_like(vpad_ref)
        kpad_ref[0:ts, :] = kn_ref[...]
        vpad_ref[0:ts, :] = vn_ref[...]
        key = lax.broadcasted_iota(jnp.int32, (page, ncol), 0)
        qry = lax.broadcasted_iota(jnp.int32, (page, ncol), 1) % _QPAD
        keep = (key < qry) & (qry < ts)
        process(kpad_ref[...], vpad_ref[...], keep)

    for r in range(pages_per_step):
        process(kv_refs[r][...], kv_refs[pages_per_step + r][...], None)

    @pl.when(s == pl.num_programs(1) - 1)
    def _():
        outs = []
        for hh in range(heads):
            blk = acc_ref[hh * _QPAD:hh * _QPAD + ts, hh * dh:(hh + 1) * dh]
            outs.append(_rms(blk, g_ref[...]))
        o_ref[...] = jnp.concatenate(outs, axis=1).astype(BF16)


def _sb_sample(proj, cache_k, cache_v, page_table, bias_cols, ut2, g, *, heads, pages_per_step=4):
    _, b, ts, w = proj.shape
    _, page, _ = cache_k.shape
    n_pages = page_table.shape[1]
    dh = w // heads
    assert ts <= _QPAD and dh == page == V7X_LANES
    gp = pages_per_step if n_pages % pages_per_step == 0 else 1
    ncol = heads * _QPAD

    def page_spec(r):
        return pl.BlockSpec(
            (None, page, w), lambda bi, s, pt: (pt[bi, n_pages - 1 - (s * gp + r)], 0, 0))

    def tok_spec(p):
        return pl.BlockSpec((None, None, ts, w), lambda bi, s, pt: (p, bi, 0, 0))

    grid_spec = pltpu.PrefetchScalarGridSpec(
        num_scalar_prefetch=1,
        grid=(b, n_pages // gp),
        in_specs=[
            tok_spec(0), tok_spec(1), tok_spec(2),
            pl.BlockSpec((1, ncol), lambda bi, s, pt: (0, 0)),
            pl.BlockSpec((page, 2 * page), lambda bi, s, pt: (0, 0)),
            pl.BlockSpec((1, dh), lambda bi, s, pt: (0, 0)),
        ] + [page_spec(r) for r in range(gp)] * 2,
        out_specs=pl.BlockSpec((None, ts, w), lambda bi, s, pt: (bi, 0, 0)),
        scratch_shapes=[
            pltpu.VMEM((ncol, w), BF16),
            pltpu.VMEM((ncol, w), F32),
            pltpu.VMEM((1, ncol), F32),
            pltpu.VMEM((page, w), F32),
            pltpu.VMEM((page, w), F32),
        ],
    )
    return pl.pallas_call(
        functools.partial(_sb_sample_kernel, pages_per_step=gp, heads=heads, scale=dh ** -0.5),
        grid_spec=grid_spec,
        out_shape=jax.ShapeDtypeStruct((b, ts, w), BF16),
        compiler_params=_params("parallel", "arbitrary"),
        name="sb_sample",
    )(page_table, proj, proj, proj, bias_cols, ut2, g,
      *([cache_k] * gp), *([cache_v] * gp))


def _gelu_tanh(x):
    return 0.5 * x * (1.0 + jnp.tanh(math.sqrt(2.0 / math.pi) * (x + 0.044715 * (x * x * x))))


def _rg_coeffs(xc, wa_ref, ba, wx_ref, bx, lam, reset=None):
    blocks = wa_ref.shape[0]
    bw = wa_ref.shape[1]
    ra, rx = [], []
    for n in range(blocks):
        xb = xc[:, n * bw:(n + 1) * bw].astype(BF16)
        ra.append(jnp.dot(xb, wa_ref[n], preferred_element_type=F32))
        rx.append(jnp.dot(xb, wx_ref[n], preferred_element_type=F32))
    r = jax.nn.sigmoid(jnp.concatenate(ra, axis=1) + ba)
    ig = jax.nn.sigmoid(jnp.concatenate(rx, axis=1) + bx)
    log_a = -RG_C * r * _softplus(-lam)
    a = jnp.exp(log_a)
    y = 2.0 * log_a
    u = jnp.exp(y)
    mid = (u < 1.0) & (u > 0.0)
    ratio = (1.0 - u) * y / jnp.log(jnp.where(mid, u, 0.5))
    neg_expm1 = jnp.where(mid, ratio, jnp.where(u > 0.0, -y, 1.0))
    mult = jnp.sqrt(neg_expm1)
    if reset is not None:
        a = jnp.where(reset, 0.0, a)
        mult = jnp.where(reset, 1.0, mult)
    return a, mult * ig * xc


def _rglru_prompt_kernel(x_ref, gr_ref, cw_ref, cb_ref, wa_ref, ba_ref, wx_ref, bx_ref, lam_ref,
                         o_ref, hlast_ref, xbuf_ref, a_ref, b_ref, h_ref, *, taps):
    step = pl.program_id(0)
    tc = x_ref.shape[0]
    pad = V7X_SUBLANES

    @pl.when(step == 0)
    def _():
        xbuf_ref[0:pad, :] = jnp.zeros((pad, x_ref.shape[1]), F32)
        h_ref[...] = jnp.zeros_like(h_ref)

    xbuf_ref[pad:pad + tc, :] = x_ref[...]
    base = pad - (taps - 1)
    xc = cb_ref[...] + xbuf_ref[base:base + tc, :] * cw_ref[0:1, :]
    for tap in range(1, taps):
        xc = xc + xbuf_ref[base + tap:base + tap + tc, :] * cw_ref[tap:tap + 1, :]
    reset = (step * tc + lax.broadcasted_iota(jnp.int32, xc.shape, 0)) == 0
    a, b = _rg_coeffs(xc, wa_ref, ba_ref[...], wx_ref, bx_ref[...], lam_ref[...], reset)
    a_ref[...] = a
    b_ref[...] = b

    def body(t, h):
        h = a_ref[pl.ds(t, 1), :] * h + b_ref[pl.ds(t, 1), :]
        b_ref[pl.ds(t, 1), :] = h
        return h

    h = lax.fori_loop(0, tc, body, h_ref[...], unroll=8)
    h_ref[...] = h
    hlast_ref[...] = h
    o_ref[...] = (_gelu_tanh(gr_ref[...]) * b_ref[...]).astype(BF16)
    xbuf_ref[0:pad, :] = xbuf_ref[tc:tc + pad, :]


def _rglru_prompt(proj, conv_w, conv_b, wa, ba, wx, bx, lam, *, x_part, gate_part, tc=512):
    _, t, w = proj.shape
    taps = conv_w.shape[0]
    tc = _row_block(t, tc)
    assert tc % V7X_SUBLANES == 0 and taps - 1 <= V7X_SUBLANES
    const2 = lambda i: (0, 0)
    const3 = lambda i: (0, 0, 0)
    return pl.pallas_call(
        functools.partial(_rglru_prompt_kernel, taps=taps),
        grid=(t // tc,),
        in_specs=[
            pl.BlockSpec((None, tc, w), lambda i: (x_part, i, 0)),
            pl.BlockSpec((None, tc, w), lambda i: (gate_part, i, 0)),
            pl.BlockSpec(conv_w.shape, const2),
            pl.BlockSpec((1, w), const2),
            pl.BlockSpec(wa.shape, const3),
            pl.BlockSpec((1, w), const2),
            pl.BlockSpec(wx.shape, const3),
            pl.BlockSpec((1, w), const2),
            pl.BlockSpec((1, w), const2),
        ],
        out_specs=[pl.BlockSpec((tc, w), lambda i: (i, 0)), pl.BlockSpec((1, w), const2)],
        out_shape=[jax.ShapeDtypeStruct((t, w), BF16), jax.ShapeDtypeStruct((1, w), F32)],
        scratch_shapes=[
            pltpu.VMEM((tc + V7X_SUBLANES, w), F32),
            pltpu.VMEM((tc, w), F32),
            pltpu.VMEM((tc, w), F32),
            pltpu.VMEM((1, w), F32),
        ],
        compiler_params=_params("arbitrary"),
        name="rglru_prompt",
    )(proj, proj, conv_w, conv_b, wa, ba, wx, bx, lam)


def _rglru_sample_kernel(x_ref, buf_ref, gr_ref, h0_ref, cw_ref, cb_ref, wa_ref, ba_ref, wx_ref,
                         bx_ref, lam_ref, o_ref, hnew_ref, *, first_pos):
    ts = x_ref.shape[0]
    taps = cw_ref.shape[0]
    rows = [buf_ref[n] for n in range(taps - 1)] + [x_ref[n] for n in range(ts)]
    h = h0_ref[...]
    for t in range(ts):
        xc = cb_ref[...] + rows[t] * cw_ref[0:1, :]
        for tap in range(1, taps):
            xc = xc + rows[t + tap] * cw_ref[tap:tap + 1, :]
        reset = jnp.full(xc.shape, True) if first_pos + t == 0 else None
        a, b = _rg_coeffs(xc, wa_ref, ba_ref[...], wx_ref, bx_ref[...], lam_ref[...], reset)
        h = a * h + b
        o_ref[t] = (_gelu_tanh(gr_ref[t]) * h).astype(BF16)
    hnew_ref[...] = h


def _rglru_sample(x_tm, buf_tm, gr_tm, h0, conv_w, conv_b, wa, ba, wx, bx, lam, *, first_pos):
    ts, b, w = x_tm.shape
    return pl.pallas_call(
        functools.partial(_rglru_sample_kernel, first_pos=first_pos),
        out_shape=[jax.ShapeDtypeStruct((ts, b, w), BF16), jax.ShapeDtypeStruct((b, w), F32)],
        compiler_params=pltpu.CompilerParams(vmem_limit_bytes=VMEM_LIMIT),
        name="rglru_sample",
    )(x_tm, buf_tm, gr_tm, h0, conv_w, conv_b, wa, ba, wx, bx, lam)


def _mem_attn_kernel(q_ref, mk_ref, mv_ref, o_ref, *, heads):
    w = q_ref.shape[-1]
    dh = w // heads
    outs = []
    for hh in range(heads):
        sl = slice(hh * dh, (hh + 1) * dh)
        qh = q_ref[:, sl].astype(BF16)
        kh = mk_ref[:, sl].astype(BF16)
        vh = mv_ref[:, sl].astype(BF16)
        s = lax.dot_general(qh, kh, _NT, preferred_element_type=F32) * (dh ** -0.5)
        e = jnp.exp(s - jnp.max(s, axis=-1, keepdims=True))
        p = e / jnp.sum(e, axis=-1, keepdims=True)
        outs.append(jnp.dot(p.astype(BF16), vh, preferred_element_type=F32))
    o_ref[...] = jnp.concatenate(outs, axis=1).astype(BF16)


def _mem_attn_prompt(proj, memkv, *, q_part, heads, tm=512):
    _, t, w = proj.shape
    mtok = memkv.shape[1]
    tm = _row_block(t, tm)
    return pl.pallas_call(
        functools.partial(_mem_attn_kernel, heads=heads),
        grid=(t // tm,),
        in_specs=[
            pl.BlockSpec((None, tm, w), lambda i: (q_part, i, 0)),
            pl.BlockSpec((None, mtok, w), lambda i: (0, 0, 0)),
            pl.BlockSpec((None, mtok, w), lambda i: (1, 0, 0)),
        ],
        out_specs=pl.BlockSpec((tm, w), lambda i: (i, 0)),
        out_shape=jax.ShapeDtypeStruct((t, w), BF16),
        compiler_params=_params("parallel"),
        name="mem_attn_prompt",
    )(proj, memkv, memkv)


def _mem_attn_sample(proj, mem_k, mem_v, *, q_part, heads):
    _, b, ts, w = proj.shape
    mtok = mem_k.shape[1]
    return pl.pallas_call(
        functools.partial(_mem_attn_kernel, heads=heads),
        grid=(b,),
        in_specs=[
            pl.BlockSpec((None, None, ts, w), lambda i: (q_part, i, 0, 0)),
            pl.BlockSpec((None, mtok, w), lambda i: (i, 0, 0)),
            pl.BlockSpec((None, mtok, w), lambda i: (i, 0, 0)),
        ],
        out_specs=pl.BlockSpec((None, ts, w), lambda i: (i, 0, 0)),
        out_shape=jax.ShapeDtypeStruct((b, ts, w), BF16),
        compiler_params=_params("parallel"),
        name="mem_attn_sample",
    )(proj, mem_k, mem_v)


def _merge_kernel(o0_ref, o1_ref, o2_ref, g0_ref, g1_ref, g2_ref, bg_ref, wb_ref, m_ref):
    acc = None
    for n, (o_ref, g_ref) in enumerate(((o0_ref, g0_ref), (o1_ref, g1_ref), (o2_ref, g2_ref))):
        gate = jax.nn.sigmoid(g_ref[...] + bg_ref[n])
        term = gate * jnp.dot(o_ref[...], wb_ref[n], preferred_element_type=F32)
        acc = term if acc is None else acc + term
    m_ref[...] = acc.astype(BF16)


def _merge(branches, proj, b_gate3, w_branch, *, gate_part0, tm=512, tn=512):
    m, w = branches[0].shape
    d = w_branch.shape[2]
    tm = _row_block(m, tm)
    per_w = w // tn
    per_branch = d // w

    def gate_spec(n):
        return pl.BlockSpec(
            (None, tm, tn),
            lambda i, j: (gate_part0 + n * per_branch + j // per_w, i, j % per_w))

    return pl.pallas_call(
        _merge_kernel,
        grid=(m // tm, d // tn),
        in_specs=[pl.BlockSpec((tm, w), lambda i, j: (i, 0))] * N_BRANCH
        + [gate_spec(n) for n in range(N_BRANCH)]
        + [pl.BlockSpec((N_BRANCH, 1, tn), lambda i, j: (0, 0, j)),
           pl.BlockSpec((N_BRANCH, w, tn), lambda i, j: (0, 0, j))],
        out_specs=pl.BlockSpec((tm, tn), lambda i, j: (i, j)),
        out_shape=jax.ShapeDtypeStruct((m, d), BF16),
        compiler_params=_params("parallel", "arbitrary"),
        name="merge",
    )(*branches, proj, proj, proj, b_gate3, w_branch)


def _out_proj_kernel(x_ref, m_ref, w_ref, g_ref, o_ref):
    y = jnp.dot(m_ref[...], w_ref[...], preferred_element_type=F32)
    o_ref[...] = x_ref[...] + _rms(y, g_ref[...])


def _out_proj(x, merged, w_out, g, *, tm=512):
    m, d = x.shape
    tm = _row_block(m, tm)
    return pl.pallas_call(
        _out_proj_kernel,
        grid=(m // tm,),
        in_specs=[
            pl.BlockSpec((tm, d), lambda i: (i, 0)),
            pl.BlockSpec((tm, d), lambda i: (i, 0)),
            pl.BlockSpec((d, d), lambda i: (0, 0)),
            pl.BlockSpec((1, d), lambda i: (0, 0)),
        ],
        out_specs=pl.BlockSpec((tm, d), lambda i: (i, 0)),
        out_shape=jax.ShapeDtypeStruct((m, d), F32),
        compiler_params=_params("parallel"),
        name="out_proj",
    )(x, merged, w_out, g)


def _strict_upper_pair(n):
    j = lax.broadcasted_iota(jnp.int32, (n, n), 0)
    s = lax.broadcasted_iota(jnp.int32, (n, n), 1)
    u = (j > s).astype(BF16)
    return jnp.concatenate([u, u], axis=0)


def kernel(x_prompt, x_sample, cache_sb_k, cache_sb_v, state_conv, state_rglru, cache_mem_k, cache_mem_v, page_table, mem_prompt, ffn1_g_pre, ffn1_g_post, ffn1_w_gu, ffn1_w_down, mix_g_pre, mix_g_post, w_in, b_gate, sb_bias, sb_norm_g, conv_w, conv_b, rg_w_a, rg_b_a, rg_w_x, rg_b_x, rg_lambda, mem_g, w_mem_kv, w_branch, w_out, ffn2_g_pre, ffn2_g_post, ffn2_w_gu, ffn2_w_down):
    bp, tp, d = x_prompt.shape
    bs, ts, _ = x_sample.shape
    assert bp == 1
    n_pool, page, sb_heads, sb_dh = cache_sb_k.shape
    mix_w = sb_heads * sb_dh
    n_pages = page_table.shape[1]
    past_len = n_pages * page
    mem_tok, mem_heads = cache_mem_k.shape[1], cache_mem_k.shape[2]

    row = lambda v: v.reshape(1, -1).astype(F32)
    bf = lambda v: v.astype(BF16)
    w1_gu, w1_down = bf(ffn1_w_gu), bf(ffn1_w_down)
    w2_gu, w2_down = bf(ffn2_w_gu), bf(ffn2_w_down)
    w_in_b, w_mem_b = bf(w_in), bf(w_mem_kv)
    w_branch_b, w_out_b = bf(w_branch), bf(w_out)
    wa_b, wx_b = bf(rg_w_a), bf(rg_w_x)
    b_gate3 = b_gate.reshape(N_BRANCH, 1, d)
    sb_g = row(sb_norm_g)
    rg_vecs = (conv_w, row(conv_b), wa_b, row(rg_b_a), wx_b, row(rg_b_x), row(rg_lambda))
    P_Q, P_K, P_V, P_X, P_GR, P_QM, P_GL = range(7)

    def ffn(x, n):
        if n == 1:
            return _ffn(x, row(ffn1_g_pre), row(ffn1_g_post), w1_gu, w1_down)
        return _ffn(x, row(ffn2_g_pre), row(ffn2_g_post), w2_gu, w2_down)

    def finish(x, branches, proj):
        merged = _merge(branches, proj, b_gate3, w_branch_b, gate_part0=P_GL)
        x = _out_proj(x, merged, w_out_b, row(mix_g_post))
        return ffn(x, 2)

    memkv = _norm_proj(mem_prompt.reshape(mem_tok, d), row(mem_g), w_mem_b, tn=mix_w)
    xp = ffn(x_prompt.reshape(tp, d), 1)
    proj_p, qkv_p = _norm_proj(xp, row(mix_g_pre), w_in_b, tn=mix_w, n_bf16=3)
    o_sb_p = _sb_prompt(qkv_p, sb_bias.astype(F32), _strict_upper_pair(V7X_MXU_DIM), sb_g,
                        heads=sb_heads)
    o_rg_p, h_p = _rglru_prompt(proj_p, *rg_vecs, x_part=P_X, gate_part=P_GR)
    o_mem_p = _mem_attn_prompt(proj_p, memkv, q_part=P_QM, heads=mem_heads)
    y_p = finish(xp, (o_sb_p, o_rg_p, o_mem_p), proj_p)

    ms = bs * ts
    xs = ffn(x_sample.reshape(ms, d), 1)
    proj_s = _norm_proj(xs, row(mix_g_pre), w_in_b, tn=mix_w)
    proj_s4 = proj_s.reshape(proj_s.shape[0], bs, ts, mix_w)
    ut = _strict_upper_pair(page)
    ut2 = jnp.concatenate([ut[:page].T, ut[:page].T], axis=1)
    bias_cols = jnp.repeat(sb_bias.astype(F32), _QPAD).reshape(1, sb_heads * _QPAD)
    o_sb_s = _sb_sample(proj_s4, cache_sb_k.reshape(n_pool, page, mix_w),
                        cache_sb_v.reshape(n_pool, page, mix_w), page_table, bias_cols, ut2, sb_g,
                        heads=sb_heads)
    to_tm = lambda a: a.reshape(bs, ts, mix_w).transpose(1, 0, 2)
    o_rg_tm, h_s = _rglru_sample(to_tm(proj_s[P_X]), state_conv.transpose(1, 0, 2),
                                 to_tm(proj_s[P_GR]), state_rglru, *rg_vecs, first_pos=past_len)
    o_rg_s = o_rg_tm.transpose(1, 0, 2)
    o_mem_s = _mem_attn_sample(proj_s4, cache_mem_k.reshape(bs, mem_tok, mix_w),
                               cache_mem_v.reshape(bs, mem_tok, mix_w), q_part=P_QM,
                               heads=mem_heads)
    branches_s = tuple(o.reshape(ms, mix_w) for o in (o_sb_s, o_rg_s, o_mem_s))
    y_s = finish(xs, branches_s, proj_s)

    n_keep = conv_w.shape[0] - 1
    xr_s = proj_s[P_X].reshape(bs, ts, mix_w)
    conv_s = jnp.concatenate([state_conv.astype(F32), xr_s], axis=1)[:, ts:]
    conv_p = proj_p[P_X][tp - n_keep:].reshape(1, n_keep, mix_w)
    heads4 = lambda a, b_, t_: a.reshape(b_, t_, sb_heads, sb_dh)
    return (
        y_p.reshape(bp, tp, d),
        y_s.reshape(bs, ts, d),
        heads4(proj_p[P_K], bp, tp), heads4(proj_p[P_V], bp, tp),
        heads4(proj_s[P_K], bs, ts), heads4(proj_s[P_V], bs, ts),
        conv_p, conv_s,
        h_p, h_s,
        memkv[0].reshape(bp, mem_tok, mem_heads, mix_w // mem_heads),
        memkv[1].reshape(bp, mem_tok, mem_heads, mix_w // mem_heads),
    )
```

```python
import functools
import math

import jax
import jax.numpy as jnp
from jax import lax
from jax.experimental import pallas as pl
from jax.experimental.pallas import tpu as pltpu

F32 = jnp.float32
BF16 = jnp.bfloat16

EPS = 1e-6
RG_C = 8.0
N_BRANCH = 3
SB_HEADS = 8
MEM_HEADS = 4
RG_BLOCKS = 8

V7X_LANES = 128
V7X_SUBLANES = 8
V7X_MXU_DIM = 256
V7X_VMEM_BYTES = 64 * 1024 * 1024
VMEM_LIMIT = V7X_VMEM_BYTES - 8 * 1024 * 1024

_LOG2E = 1.4426950408889634
_NT = (((1,), (1,)), ((), ()))
_TN = (((0,), (0,)), ((), ()))


def _params(*sem):
    return pltpu.CompilerParams(dimension_semantics=sem, vmem_limit_bytes=VMEM_LIMIT)


def _rms(x, g):
    ms = jnp.mean(x * x, axis=-1, keepdims=True)
    return x * lax.rsqrt(ms + EPS) * g


def _softplus(z):
    return jnp.maximum(z, 0.0) + jnp.log(1.0 + jnp.exp2(jnp.abs(z) * (-_LOG2E)))


def _split_bf16(x):
    hi = x.astype(BF16)
    lo = (x - hi.astype(F32)).astype(BF16)
    return hi, lo


def _row_block(m, target):
    return target if m % target == 0 else m


def _ffn_kernel(x_ref, gpre_ref, gpost_ref, wg_ref, wu_ref, wd_ref, o_ref, h_ref):
    j = pl.program_id(1)

    @pl.when(j == 0)
    def _():
        h_ref[...] = _rms(x_ref[...], gpre_ref[...]).astype(BF16)
        o_ref[...] = jnp.zeros_like(o_ref)

    h = h_ref[...]
    gate = jnp.dot(h, wg_ref[...], preferred_element_type=F32)
    up = jnp.dot(h, wu_ref[...], preferred_element_type=F32)
    act = (gate * jax.nn.sigmoid(gate) * up).astype(BF16)
    o_ref[...] += jnp.dot(act, wd_ref[...], preferred_element_type=F32)

    @pl.when(j == pl.num_programs(1) - 1)
    def _():
        o_ref[...] = x_ref[...] + 0.5 * _rms(o_ref[...], gpost_ref[...])


def _ffn(x, g_pre, g_post, w_gu, w_down, *, tm=512, tf=512):
    m, d = x.shape
    f = w_down.shape[0]
    tm = _row_block(m, tm)
    nf = f // tf
    return pl.pallas_call(
        _ffn_kernel,
        grid=(m // tm, nf),
        in_specs=[
            pl.BlockSpec((tm, d), lambda i, j: (i, 0)),
            pl.BlockSpec((1, d), lambda i, j: (0, 0)),
            pl.BlockSpec((1, d), lambda i, j: (0, 0)),
            pl.BlockSpec((d, tf), lambda i, j: (0, j)),
            pl.BlockSpec((d, tf), lambda i, j: (0, j + nf)),
            pl.BlockSpec((tf, d), lambda i, j: (j, 0)),
        ],
        out_specs=pl.BlockSpec((tm, d), lambda i, j: (i, 0)),
        out_shape=jax.ShapeDtypeStruct((m, d), F32),
        scratch_shapes=[pltpu.VMEM((tm, d), BF16)],
        compiler_params=_params("parallel", "arbitrary"),
        name="ffn",
    )(x, g_pre, g_post, w_gu, w_gu, w_down)


def _norm_proj_kernel(x_ref, g_ref, w_ref, o_ref, *rest, n_bf16):
    if n_bf16:
        obf_ref, h_ref = rest
    else:
        (h_ref,) = rest
    j = pl.program_id(1)

    @pl.when(j == 0)
    def _():
        h_ref[...] = _rms(x_ref[...], g_ref[...]).astype(BF16)

    y = jnp.dot(h_ref[...], w_ref[...], preferred_element_type=F32)
    o_ref[...] = y
    if n_bf16:
        @pl.when(j < n_bf16)
        def _():
            obf_ref[...] = y.astype(BF16)


def _norm_proj(x, g, w, *, tn, n_bf16=0, tm=1024):
    m, d = x.shape
    n = w.shape[1]
    tm = _row_block(m, tm)
    parts = n // tn
    out_shape = [jax.ShapeDtypeStruct((parts, m, tn), F32)]
    out_specs = [pl.BlockSpec((None, tm, tn), lambda i, j: (j, i, 0))]
    if n_bf16:
        out_shape.append(jax.ShapeDtypeStruct((n_bf16, m, tn), BF16))
        out_specs.append(
            pl.BlockSpec((None, tm, tn), lambda i, j: (jnp.minimum(j, n_bf16 - 1), i, 0)))
    res = pl.pallas_call(
        functools.partial(_norm_proj_kernel, n_bf16=n_bf16),
        grid=(m // tm, parts),
        in_specs=[
            pl.BlockSpec((tm, d), lambda i, j: (i, 0)),
            pl.BlockSpec((1, d), lambda i, j: (0, 0)),
            pl.BlockSpec((d, tn), lambda i, j: (0, j)),
        ],
        out_specs=out_specs,
        out_shape=out_shape,
        scratch_shapes=[pltpu.VMEM((tm, d), BF16)],
        compiler_params=_params("parallel", "arbitrary"),
        name="norm_proj",
    )(x, g, w)
    return res if n_bf16 else res[0]


def _sb_prompt_kernel(bias_ref, q_ref, k_ref, v_ref, u2_ref, ui_ref, g_ref, o_ref, acc_ref, c_ref,
                      *, bq, bk, scale):
    h = pl.program_id(0)
    i = pl.program_id(1)
    bias = bias_ref[h]
    q = q_ref[...]
    r = bq // bk
    sp0 = _softplus(jnp.full((1, bk), bias, F32))
    col_const = sp0 * (bk - lax.broadcasted_iota(jnp.int32, (1, bk), 1)).astype(F32)

    def scores(kb):
        start = pl.multiple_of(kb * bk, bk)
        z = lax.dot_general(q, k_ref[pl.ds(start, bk), :], _NT, preferred_element_type=F32)
        return z * scale + bias, v_ref[pl.ds(start, bk), :]

    def diag_block(kb, c):
        z, vblk = scores(kb)
        row = i * bq + lax.broadcasted_iota(jnp.int32, (bq, bk), 0)
        col = kb * bk + lax.broadcasted_iota(jnp.int32, (bq, bk), 1)
        keep = col < row
        sp = jnp.where(keep, _softplus(z), 0.0)
        hi, lo = _split_bf16(sp)
        excl = jnp.dot(jnp.concatenate([hi, lo], axis=1), u2_ref[...], preferred_element_type=F32)
        w = jnp.where(keep, jnp.exp(z - sp - excl - c), 0.0)
        pv = jnp.dot(w.astype(BF16), vblk, preferred_element_type=F32)
        return pv, jnp.sum(sp, axis=1, keepdims=True)

    def plain_block(kb, c):
        z, vblk = scores(kb)
        d = (_softplus(z) - sp0).astype(BF16)
        incl = jnp.dot(d, ui_ref[...], preferred_element_type=F32)
        w = jnp.exp(z - incl - col_const - c)
        pv = jnp.dot(w.astype(BF16), vblk, preferred_element_type=F32)
        return pv, incl[:, 0:1] + col_const[:, 0:1]

    acc = jnp.zeros(acc_ref.shape, F32)
    c = jnp.zeros(c_ref.shape, F32)
    for d in range(r):
        pv, tot = diag_block(i * r + (r - 1 - d), c)
        acc = acc + pv
        c = c + tot
    acc_ref[...] = acc
    c_ref[...] = c

    def body(n, carry):
        c = c_ref[...]
        pvs = None
        for d in range(r):
            pv, tot = plain_block((i - n) * r - 1 - d, c)
            pvs = pv if pvs is None else pvs + pv
            c = c + tot
        acc_ref[...] += pvs
        c_ref[...] = c
        return carry

    lax.fori_loop(0, i, body, 0)
    o_ref[...] = _rms(acc_ref[...], g_ref[...]).astype(BF16)


def _sb_prompt(qkv, bias, u_strict2, u_incl, g, *, heads, bq=1024, bk=V7X_MXU_DIM):
    _, t, w = qkv.shape
    dh = w // heads
    bq = min(bq, t)
    assert t % bq == 0 and bq % bk == 0
    return pl.pallas_call(
        functools.partial(_sb_prompt_kernel, bq=bq, bk=bk, scale=dh ** -0.5),
        grid=(heads, t // bq),
        in_specs=[
            pl.BlockSpec(memory_space=pltpu.SMEM),
            pl.BlockSpec((None, bq, dh), lambda h, i: (0, i, h)),
            pl.BlockSpec((None, t, dh), lambda h, i: (1, 0, h)),
            pl.BlockSpec((None, t, dh), lambda h, i: (2, 0, h)),
            pl.BlockSpec((2 * bk, bk), lambda h, i: (0, 0)),
            pl.BlockSpec((bk, bk), lambda h, i: (0, 0)),
            pl.BlockSpec((1, dh), lambda h, i: (0, 0)),
        ],
        out_specs=pl.BlockSpec((bq, dh), lambda h, i: (i, h)),
        out_shape=jax.ShapeDtypeStruct((t, w), BF16),
        scratch_shapes=[pltpu.VMEM((bq, dh), F32), pltpu.VMEM((bq, 1), F32)],
        compiler_params=_params("parallel", "arbitrary"),
        name="sb_prompt",
    )(bias, qkv, qkv, qkv, u_strict2, u_incl, g)


_QPAD = 8


def _sb_sample_kernel(pt_ref, q_ref, kn_ref, vn_ref, bias_ref, ut_ref, g_ref, *rest,
                      pages_per_step, heads, scale):
    kv_refs = rest[:2 * pages_per_step]
    o_ref, qrows_ref, acc_ref, c_ref, kpad_ref, vpad_ref = rest[2 * pages_per_step:]
    del pt_ref
    s = pl.program_id(1)
    ts, w = q_ref.shape
    dh = w // heads
    ncol = heads * _QPAD
    page = kpad_ref.shape[0]

    def by_key(ref):
        parts = [ref[pl.ds(hh, page, stride=heads), :] for hh in range(heads)]
        return jnp.concatenate(parts, axis=1).astype(BF16)

    def log_weights(kbs, keep=None):
        zs = [lax.dot_general(kb, qrows_ref[...], _NT, preferred_element_type=F32) * scale
              + bias_ref[...] for kb in kbs]
        sps = [_softplus(z) for z in zs]
        if keep is not None:
            sps = [jnp.where(keep, sp, 0.0) for sp in sps]
        excls = [jnp.dot(ut_ref[...], jnp.concatenate(_split_bf16(sp), axis=0),
                         preferred_element_type=F32) for sp in sps]
        return ([z - sp - excl for z, sp, excl in zip(zs, sps, excls)],
                [jnp.sum(sp, axis=0, keepdims=True) for sp in sps])

    @pl.when(s == 0)
    def _():
        q8 = jnp.concatenate([q_ref[...], jnp.zeros((_QPAD - ts, w), F32)], axis=0)
        qt = jnp.concatenate([q8] * heads, axis=0)
        row_head = lax.broadcasted_iota(jnp.int32, (ncol, w), 0) // _QPAD
        col_head = lax.broadcasted_iota(jnp.int32, (ncol, w), 1) // dh
        qrows_ref[...] = jnp.where(row_head == col_head, qt, 0.0).astype(BF16)
        kpad_ref[...] = jnp.zeros_like(kpad_ref)
        vpad_ref[...] = jnp.zeros_like(vpad_ref)
        kpad_ref[0:ts, :] = kn_ref[...]
        vpad_ref[0:ts, :] = vn_ref[...]
        key = lax.broadcasted_iota(jnp.int32, (page, ncol), 0)
        qry = lax.broadcasted_iota(jnp.int32, (page, ncol), 1) % _QPAD
        keep = (key < qry) & (qry < ts)
        (arg,), (tot,) = log_weights([kpad_ref[...].astype(BF16)], keep)
        wgt = jnp.where(keep, jnp.exp(arg), 0.0).astype(BF16)
        acc_ref[...] = lax.dot_general(wgt, vpad_ref[...].astype(BF16), _TN,
                                       preferred_element_type=F32)
        c_ref[...] = tot

    args, tots = log_weights([by_key(kv_refs[r]) for r in range(pages_per_step)])
    c = c_ref[...]
    wgts = []
    for arg, tot in zip(args, tots):
        wgts.append(jnp.exp(arg - c).astype(BF16))
        c = c + tot
    c_ref[...] = c
    vals = [by_key(kv_refs[pages_per_step + r]) for r in range(pages_per_step)]
    acc_ref[...] += lax.dot_general(jnp.concatenate(wgts, axis=0), jnp.concatenate(vals, axis=0),
                                    _TN, preferred_element_type=F32)

    @pl.when(s == pl.num_programs(1) - 1)
    def _():
        outs = []
        for hh in range(heads):
            blk = acc_ref[hh * _QPAD:hh * _QPAD + ts, hh * dh:(hh + 1) * dh]
            outs.append(_rms(blk, g_ref[...]))
        o_ref[...] = jnp.concatenate(outs, axis=1).astype(BF16)


def _sb_sample(proj, cache_k, cache_v, page_table, bias_cols, ut2, g, *, heads, pages_per_step=8):
    _, b, ts, w = proj.shape
    dh = w // heads
    page = cache_k.shape[1] // heads
    n_pages = page_table.shape[1]
    assert ts <= _QPAD and dh == V7X_LANES and cache_k.shape[2] == dh
    gp = pages_per_step if n_pages % pages_per_step == 0 else 1
    ncol = heads * _QPAD

    def page_spec(r):
        return pl.BlockSpec(
            (None, page * heads, dh),
            lambda bi, s, pt: (pt[bi, n_pages - 1 - (s * gp + r)], 0, 0))

    def tok_spec(p):
        return pl.BlockSpec((None, None, ts, w), lambda bi, s, pt: (p, bi, 0, 0))

    grid_spec = pltpu.PrefetchScalarGridSpec(
        num_scalar_prefetch=1,
        grid=(b, n_pages // gp),
        in_specs=[
            tok_spec(0), tok_spec(1), tok_spec(2),
            pl.BlockSpec((1, ncol), lambda bi, s, pt: (0, 0)),
            pl.BlockSpec((page, 2 * page), lambda bi, s, pt: (0, 0)),
            pl.BlockSpec((1, dh), lambda bi, s, pt: (0, 0)),
        ] + [page_spec(r) for r in range(gp)] * 2,
        out_specs=pl.BlockSpec((None, ts, w), lambda bi, s, pt: (bi, 0, 0)),
        scratch_shapes=[
            pltpu.VMEM((ncol, w), BF16),
            pltpu.VMEM((ncol, w), F32),
            pltpu.VMEM((1, ncol), F32),
            pltpu.VMEM((page, w), F32),
            pltpu.VMEM((page, w), F32),
        ],
    )
    return pl.pallas_call(
        functools.partial(_sb_sample_kernel, pages_per_step=gp, heads=heads, scale=dh ** -0.5),
        grid_spec=grid_spec,
        out_shape=jax.ShapeDtypeStruct((b, ts, w), BF16),
        compiler_params=_params("parallel", "arbitrary"),
        name="sb_sample",
    )(page_table, proj, proj, proj, bias_cols, ut2, g,
      *([cache_k] * gp), *([cache_v] * gp))


def _gelu_tanh(x):
    return 0.5 * x * (1.0 + jnp.tanh(math.sqrt(2.0 / math.pi) * (x + 0.044715 * (x * x * x))))


def _rg_coeffs(xc, wa_ref, ba, wx_ref, bx, lam, reset=None):
    blocks = wa_ref.shape[0]
    bw = wa_ref.shape[1]
    ra, rx = [], []
    for n in range(blocks):
        xb = xc[:, n * bw:(n + 1) * bw].astype(BF16)
        ra.append(jnp.dot(xb, wa_ref[n], preferred_element_type=F32))
        rx.append(jnp.dot(xb, wx_ref[n], preferred_element_type=F32))
    r = jax.nn.sigmoid(jnp.concatenate(ra, axis=1) + ba)
    ig = jax.nn.sigmoid(jnp.concatenate(rx, axis=1) + bx)
    log_a = -RG_C * r * _softplus(-lam)
    a = jnp.exp(log_a)
    y = 2.0 * log_a
    u = jnp.exp(y)
    mid = (u < 1.0) & (u > 0.0)
    ratio = (1.0 - u) * y / jnp.log(jnp.where(mid, u, 0.5))
    neg_expm1 = jnp.where(mid, ratio, jnp.where(u > 0.0, -y, 1.0))
    mult = jnp.sqrt(neg_expm1)
    if reset is not None:
        a = jnp.where(reset, 0.0, a)
        mult = jnp.where(reset, 1.0, mult)
    return a, mult * ig * xc


def _rglru_prompt_kernel(x_ref, gr_ref, cw_ref, cb_ref, wa_ref, ba_ref, wx_ref, bx_ref, lam_ref,
                         o_ref, hlast_ref, xbuf_ref, a_ref, b_ref, h_ref, *, taps):
    step = pl.program_id(0)
    tc = x_ref.shape[0]
    pad = V7X_SUBLANES

    @pl.when(step == 0)
    def _():
        xbuf_ref[0:pad, :] = jnp.zeros((pad, x_ref.shape[1]), F32)
        h_ref[...] = jnp.zeros_like(h_ref)

    xbuf_ref[pad:pad + tc, :] = x_ref[...]
    base = pad - (taps - 1)
    xc = cb_ref[...] + xbuf_ref[base:base + tc, :] * cw_ref[0:1, :]
    for tap in range(1, taps):
        xc = xc + xbuf_ref[base + tap:base + tap + tc, :] * cw_ref[tap:tap + 1, :]
    reset = (step * tc + lax.broadcasted_iota(jnp.int32, xc.shape, 0)) == 0
    a, b = _rg_coeffs(xc, wa_ref, ba_ref[...], wx_ref, bx_ref[...], lam_ref[...], reset)
    a_ref[...] = a
    b_ref[...] = b

    def body(t, h):
        h = a_ref[pl.ds(t, 1), :] * h + b_ref[pl.ds(t, 1), :]
        b_ref[pl.ds(t, 1), :] = h
        return h

    h = lax.fori_loop(0, tc, body, h_ref[...], unroll=8)
    h_ref[...] = h
    hlast_ref[...] = h
    o_ref[...] = (_gelu_tanh(gr_ref[...]) * b_ref[...]).astype(BF16)
    xbuf_ref[0:pad, :] = xbuf_ref[tc:tc + pad, :]


def _rglru_prompt(proj, conv_w, conv_b, wa, ba, wx, bx, lam, *, x_part, gate_part, tc=512):
    _, t, w = proj.shape
    taps = conv_w.shape[0]
    tc = _row_block(t, tc)
    assert tc % V7X_SUBLANES == 0 and taps - 1 <= V7X_SUBLANES
    const2 = lambda i: (0, 0)
    const3 = lambda i: (0, 0, 0)
    return pl.pallas_call(
        functools.partial(_rglru_prompt_kernel, taps=taps),
        grid=(t // tc,),
        in_specs=[
            pl.BlockSpec((None, tc, w), lambda i: (x_part, i, 0)),
            pl.BlockSpec((None, tc, w), lambda i: (gate_part, i, 0)),
            pl.BlockSpec(conv_w.shape, const2),
            pl.BlockSpec((1, w), const2),
            pl.BlockSpec(wa.shape, const3),
            pl.BlockSpec((1, w), const2),
            pl.BlockSpec(wx.shape, const3),
            pl.BlockSpec((1, w), const2),
            pl.BlockSpec((1, w), const2),
        ],
        out_specs=[pl.BlockSpec((tc, w), lambda i: (i, 0)), pl.BlockSpec((1, w), const2)],
        out_shape=[jax.ShapeDtypeStruct((t, w), BF16), jax.ShapeDtypeStruct((1, w), F32)],
        scratch_shapes=[
            pltpu.VMEM((tc + V7X_SUBLANES, w), F32),
            pltpu.VMEM((tc, w), F32),
            pltpu.VMEM((tc, w), F32),
            pltpu.VMEM((1, w), F32),
        ],
        compiler_params=_params("arbitrary"),
        name="rglru_prompt",
    )(proj, proj, conv_w, conv_b, wa, ba, wx, bx, lam)


def _rglru_sample_kernel(x_ref, buf_ref, gr_ref, h0_ref, cw_ref, cb_ref, wa_ref, ba_ref, wx_ref,
                         bx_ref, lam_ref, o_ref, hnew_ref, *, first_pos):
    ts = x_ref.shape[0]
    taps = cw_ref.shape[0]
    rows = [buf_ref[n] for n in range(taps - 1)] + [x_ref[n] for n in range(ts)]
    h = h0_ref[...]
    for t in range(ts):
        xc = cb_ref[...] + rows[t] * cw_ref[0:1, :]
        for tap in range(1, taps):
            xc = xc + rows[t + tap] * cw_ref[tap:tap + 1, :]
        reset = jnp.full(xc.shape, True) if first_pos + t == 0 else None
        a, b = _rg_coeffs(xc, wa_ref, ba_ref[...], wx_ref, bx_ref[...], lam_ref[...], reset)
        h = a * h + b
        o_ref[t] = (_gelu_tanh(gr_ref[t]) * h).astype(BF16)
    hnew_ref[...] = h


def _rglru_sample(x_tm, buf_tm, gr_tm, h0, conv_w, conv_b, wa, ba, wx, bx, lam, *, first_pos):
    ts, b, w = x_tm.shape
    return pl.pallas_call(
        functools.partial(_rglru_sample_kernel, first_pos=first_pos),
        out_shape=[jax.ShapeDtypeStruct((ts, b, w), BF16), jax.ShapeDtypeStruct((b, w), F32)],
        compiler_params=pltpu.CompilerParams(vmem_limit_bytes=VMEM_LIMIT),
        name="rglru_sample",
    )(x_tm, buf_tm, gr_tm, h0, conv_w, conv_b, wa, ba, wx, bx, lam)


def _mem_attn_kernel(q_ref, mk_ref, mv_ref, o_ref, *, heads):
    w = q_ref.shape[-1]
    dh = w // heads
    outs = []
    for hh in range(heads):
        sl = slice(hh * dh, (hh + 1) * dh)
        qh = q_ref[:, sl].astype(BF16)
        kh = mk_ref[:, sl].astype(BF16)
        vh = mv_ref[:, sl].astype(BF16)
        s = lax.dot_general(qh, kh, _NT, preferred_element_type=F32) * (dh ** -0.5)
        e = jnp.exp(s - jnp.max(s, axis=-1, keepdims=True))
        p = e / jnp.sum(e, axis=-1, keepdims=True)
        outs.append(jnp.dot(p.astype(BF16), vh, preferred_element_type=F32))
    o_ref[...] = jnp.concatenate(outs, axis=1).astype(BF16)


def _mem_attn_prompt(proj, memkv, *, q_part, heads, tm=512):
    _, t, w = proj.shape
    mtok = memkv.shape[1]
    tm = _row_block(t, tm)
    return pl.pallas_call(
        functools.partial(_mem_attn_kernel, heads=heads),
        grid=(t // tm,),
        in_specs=[
            pl.BlockSpec((None, tm, w), lambda i: (q_part, i, 0)),
            pl.BlockSpec((None, mtok, w), lambda i: (0, 0, 0)),
            pl.BlockSpec((None, mtok, w), lambda i: (1, 0, 0)),
        ],
        out_specs=pl.BlockSpec((tm, w), lambda i: (i, 0)),
        out_shape=jax.ShapeDtypeStruct((t, w), BF16),
        compiler_params=_params("parallel"),
        name="mem_attn_prompt",
    )(proj, memkv, memkv)


def _mem_attn_sample(proj, mem_k, mem_v, *, q_part, heads):
    _, b, ts, w = proj.shape
    mtok = mem_k.shape[1]
    return pl.pallas_call(
        functools.partial(_mem_attn_kernel, heads=heads),
        grid=(b,),
        in_specs=[
            pl.BlockSpec((None, None, ts, w), lambda i: (q_part, i, 0, 0)),
            pl.BlockSpec((None, mtok, w), lambda i: (i, 0, 0)),
            pl.BlockSpec((None, mtok, w), lambda i: (i, 0, 0)),
        ],
        out_specs=pl.BlockSpec((None, ts, w), lambda i: (i, 0, 0)),
        out_shape=jax.ShapeDtypeStruct((b, ts, w), BF16),
        compiler_params=_params("parallel"),
        name="mem_attn_sample",
    )(proj, mem_k, mem_v)


def _merge_kernel(*refs, n_branch, gate_parts):
    o_refs = refs[:n_branch]
    g_refs = refs[n_branch:n_branch + n_branch * gate_parts]
    bg_ref, wb_ref, m_ref = refs[n_branch + n_branch * gate_parts:]
    w = g_refs[0].shape[1]
    for part in range(gate_parts):
        cols = slice(part * w, (part + 1) * w)
        acc = None
        for n in range(n_branch):
            gate = jax.nn.sigmoid(g_refs[n * gate_parts + part][...] + bg_ref[n][:, cols])
            term = gate * jnp.dot(o_refs[n][...], wb_ref[n, :, cols], preferred_element_type=F32)
            acc = term if acc is None else acc + term
        m_ref[:, cols] = acc.astype(BF16)


def _merge(branches, proj, b_gate3, w_branch, *, gate_part0, tm=256):
    m, w = branches[0].shape
    n_branch, _, d = w_branch.shape
    tm = _row_block(m, tm)
    gate_parts = d // w

    def gate_spec(p):
        return pl.BlockSpec((None, tm, w), lambda i: (gate_part0 + p, i, 0))

    resident = dict(pipeline_mode=pl.Buffered(1))
    return pl.pallas_call(
        functools.partial(_merge_kernel, n_branch=n_branch, gate_parts=gate_parts),
        grid=(m // tm,),
        in_specs=[pl.BlockSpec((tm, w), lambda i: (i, 0))] * n_branch
        + [gate_spec(p) for p in range(n_branch * gate_parts)]
        + [pl.BlockSpec((n_branch, 1, d), lambda i: (0, 0, 0), **resident),
           pl.BlockSpec((n_branch, w, d), lambda i: (0, 0, 0), **resident)],
        out_specs=pl.BlockSpec((tm, d), lambda i: (i, 0)),
        out_shape=jax.ShapeDtypeStruct((m, d), BF16),
        compiler_params=_params("parallel"),
        name="merge",
    )(*branches, *([proj] * (n_branch * gate_parts)), b_gate3, w_branch)


def _out_proj_kernel(x_ref, m_ref, w_ref, g_ref, o_ref):
    y = jnp.dot(m_ref[...], w_ref[...], preferred_element_type=F32)
    o_ref[...] = x_ref[...] + _rms(y, g_ref[...])


def _out_proj(x, merged, w_out, g, *, tm=512):
    m, d = x.shape
    tm = _row_block(m, tm)
    return pl.pallas_call(
        _out_proj_kernel,
        grid=(m // tm,),
        in_specs=[
            pl.BlockSpec((tm, d), lambda i: (i, 0)),
            pl.BlockSpec((tm, d), lambda i: (i, 0)),
            pl.BlockSpec((d, d), lambda i: (0, 0)),
            pl.BlockSpec((1, d), lambda i: (0, 0)),
        ],
        out_specs=pl.BlockSpec((tm, d), lambda i: (i, 0)),
        out_shape=jax.ShapeDtypeStruct((m, d), F32),
        compiler_params=_params("parallel"),
        name="out_proj",
    )(x, merged, w_out, g)


def _strict_upper_pair(n):
    j = lax.broadcasted_iota(jnp.int32, (n, n), 0)
    s = lax.broadcasted_iota(jnp.int32, (n, n), 1)
    u = (j > s).astype(BF16)
    return jnp.concatenate([u, u], axis=0)


def kernel(x_prompt, x_sample, cache_sb_k, cache_sb_v, state_conv, state_rglru, cache_mem_k, cache_mem_v, page_table, mem_prompt, ffn1_g_pre, ffn1_g_post, ffn1_w_gu, ffn1_w_down, mix_g_pre, mix_g_post, w_in, b_gate, sb_bias, sb_norm_g, conv_w, conv_b, rg_w_a, rg_b_a, rg_w_x, rg_b_x, rg_lambda, mem_g, w_mem_kv, w_branch, w_out, ffn2_g_pre, ffn2_g_post, ffn2_w_gu, ffn2_w_down):
    bp, tp, d = x_prompt.shape
    bs, ts, _ = x_sample.shape
    assert bp == 1
    n_pool, page, sb_heads, sb_dh = cache_sb_k.shape
    mix_w = sb_heads * sb_dh
    n_pages = page_table.shape[1]
    past_len = n_pages * page
    mem_tok, mem_heads = cache_mem_k.shape[1], cache_mem_k.shape[2]

    row = lambda v: v.reshape(1, -1).astype(F32)
    bf = lambda v: v.astype(BF16)
    w1_gu, w1_down = bf(ffn1_w_gu), bf(ffn1_w_down)
    w2_gu, w2_down = bf(ffn2_w_gu), bf(ffn2_w_down)
    w_in_b, w_mem_b = bf(w_in), bf(w_mem_kv)
    w_branch_b, w_out_b = bf(w_branch), bf(w_out)
    wa_b, wx_b = bf(rg_w_a), bf(rg_w_x)
    b_gate3 = b_gate.reshape(N_BRANCH, 1, d)
    sb_g = row(sb_norm_g)
    rg_vecs = (conv_w, row(conv_b), wa_b, row(rg_b_a), wx_b, row(rg_b_x), row(rg_lambda))
    P_Q, P_K, P_V, P_X, P_GR, P_QM, P_GL = range(7)

    def ffn(x, n):
        if n == 1:
            return _ffn(x, row(ffn1_g_pre), row(ffn1_g_post), w1_gu, w1_down)
        return _ffn(x, row(ffn2_g_pre), row(ffn2_g_post), w2_gu, w2_down)

    def finish(x, branches, proj):
        merged = _merge(branches, proj, b_gate3, w_branch_b, gate_part0=P_GL)
        x = _out_proj(x, merged, w_out_b, row(mix_g_post))
        return ffn(x, 2)

    memkv = _norm_proj(mem_prompt.reshape(mem_tok, d), row(mem_g), w_mem_b, tn=mix_w)
    xp = ffn(x_prompt.reshape(tp, d), 1)
    proj_p, qkv_p = _norm_proj(xp, row(mix_g_pre), w_in_b, tn=mix_w, n_bf16=3)
    u_strict2 = _strict_upper_pair(V7X_MXU_DIM)
    u_incl = u_strict2[:V7X_MXU_DIM] + jnp.eye(V7X_MXU_DIM, dtype=BF16)
    o_sb_p = _sb_prompt(qkv_p, sb_bias.astype(F32), u_strict2, u_incl, sb_g, heads=sb_heads)
    o_rg_p, h_p = _rglru_prompt(proj_p, *rg_vecs, x_part=P_X, gate_part=P_GR)
    o_mem_p = _mem_attn_prompt(proj_p, memkv, q_part=P_QM, heads=mem_heads)
    y_p = finish(xp, (o_sb_p, o_rg_p, o_mem_p), proj_p)

    ms = bs * ts
    xs = ffn(x_sample.reshape(ms, d), 1)
    proj_s = _norm_proj(xs, row(mix_g_pre), w_in_b, tn=mix_w)
    proj_s4 = proj_s.reshape(proj_s.shape[0], bs, ts, mix_w)
    ut = _strict_upper_pair(page)
    ut2 = jnp.concatenate([ut[:page].T, ut[:page].T], axis=1)
    bias_cols = jnp.repeat(sb_bias.astype(F32), _QPAD).reshape(1, sb_heads * _QPAD)
    o_sb_s = _sb_sample(proj_s4, cache_sb_k.reshape(n_pool, page * sb_heads, sb_dh),
                        cache_sb_v.reshape(n_pool, page * sb_heads, sb_dh), page_table, bias_cols,
                        ut2, sb_g, heads=sb_heads)
    to_tm = lambda a: a.reshape(bs, ts, mix_w).transpose(1, 0, 2)
    o_rg_tm, h_s = _rglru_sample(to_tm(proj_s[P_X]), state_conv.transpose(1, 0, 2),
                                 to_tm(proj_s[P_GR]), state_rglru, *rg_vecs, first_pos=past_len)
    o_rg_s = o_rg_tm.transpose(1, 0, 2)
    o_mem_s = _mem_attn_sample(proj_s4, cache_mem_k.reshape(bs, mem_tok, mix_w),
                               cache_mem_v.reshape(bs, mem_tok, mix_w), q_part=P_QM,
                               heads=mem_heads)
    branches_s = tuple(o.reshape(ms, mix_w) for o in (o_sb_s, o_rg_s, o_mem_s))
    y_s = finish(xs, branches_s, proj_s)

    n_keep = conv_w.shape[0] - 1
    xr_s = proj_s[P_X].reshape(bs, ts, mix_w)
    conv_s = jnp.concatenate([state_conv.astype(F32), xr_s], axis=1)[:, ts:]
    conv_p = proj_p[P_X][tp - n_keep:].reshape(1, n_keep, mix_w)
    heads4 = lambda a, b_, t_: a.reshape(b_, t_, sb_heads, sb_dh)
    return (
        y_p.reshape(bp, tp, d),
        y_s.reshape(bs, ts, d),
        heads4(proj_p[P_K], bp, tp), heads4(proj_p[P_V], bp, tp),
        heads4(proj_s[P_K], bs, ts), heads4(proj_s[P_V], bs, ts),
        conv_p, conv_s,
        h_p, h_s,
        memkv[0].reshape(bp, mem_tok, mem_heads, mix_w // mem_heads),
        memkv[1].reshape(bp, mem_tok, mem_heads, mix_w // mem_heads),
    )
```

```python
import functools
import math

import jax
import jax.numpy as jnp
from jax import lax
from jax.experimental import pallas as pl
from jax.experimental.pallas import tpu as pltpu

F32 = jnp.float32
BF16 = jnp.bfloat16

EPS = 1e-6
RG_C = 8.0
N_BRANCH = 3
SB_HEADS = 8
MEM_HEADS = 4
RG_BLOCKS = 8

V7X_LANES = 128
V7X_SUBLANES = 8
V7X_MXU_DIM = 256
V7X_VMEM_BYTES = 64 * 1024 * 1024
VMEM_LIMIT = V7X_VMEM_BYTES - 8 * 1024 * 1024

_LOG2E = 1.4426950408889634
_NT = (((1,), (1,)), ((), ()))
_TN = (((0,), (0,)), ((), ()))


def _params(*sem):
    return pltpu.CompilerParams(dimension_semantics=sem, vmem_limit_bytes=VMEM_LIMIT)


def _rms(x, g):
    ms = jnp.mean(x * x, axis=-1, keepdims=True)
    return x * lax.rsqrt(ms + EPS) * g


def _softplus(z):
    return jnp.maximum(z, 0.0) + jnp.log(1.0 + jnp.exp2(jnp.abs(z) * (-_LOG2E)))


def _split_bf16(x):
    hi = x.astype(BF16)
    lo = (x - hi.astype(F32)).astype(BF16)
    return hi, lo


def _row_block(m, target):
    return target if m % target == 0 else m


def _ffn_kernel(x_ref, gpre_ref, gpost_ref, wg_ref, wu_ref, wd_ref, o_ref, h_ref):
    j = pl.program_id(1)

    @pl.when(j == 0)
    def _():
        h_ref[...] = _rms(x_ref[...], gpre_ref[...]).astype(BF16)
        o_ref[...] = jnp.zeros_like(o_ref)

    h = h_ref[...]
    gate = jnp.dot(h, wg_ref[...], preferred_element_type=F32)
    up = jnp.dot(h, wu_ref[...], preferred_element_type=F32)
    act = (gate * jax.nn.sigmoid(gate) * up).astype(BF16)
    o_ref[...] += jnp.dot(act, wd_ref[...], preferred_element_type=F32)

    @pl.when(j == pl.num_programs(1) - 1)
    def _():
        o_ref[...] = x_ref[...] + 0.5 * _rms(o_ref[...], gpost_ref[...])


def _ffn(x, g_pre, g_post, w_gu, w_down, *, tm=512, tf=512):
    m, d = x.shape
    f = w_down.shape[0]
    tm = _row_block(m, tm)
    nf = f // tf
    return pl.pallas_call(
        _ffn_kernel,
        grid=(m // tm, nf),
        in_specs=[
            pl.BlockSpec((tm, d), lambda i, j: (i, 0)),
            pl.BlockSpec((1, d), lambda i, j: (0, 0)),
            pl.BlockSpec((1, d), lambda i, j: (0, 0)),
            pl.BlockSpec((d, tf), lambda i, j: (0, j)),
            pl.BlockSpec((d, tf), lambda i, j: (0, j + nf)),
            pl.BlockSpec((tf, d), lambda i, j: (j, 0)),
        ],
        out_specs=pl.BlockSpec((tm, d), lambda i, j: (i, 0)),
        out_shape=jax.ShapeDtypeStruct((m, d), F32),
        scratch_shapes=[pltpu.VMEM((tm, d), BF16)],
        compiler_params=_params("parallel", "arbitrary"),
        name="ffn",
    )(x, g_pre, g_post, w_gu, w_gu, w_down)


def _norm_proj_kernel(x_ref, g_ref, w_ref, o_ref, *rest, n_bf16, bf16_scale0):
    if n_bf16:
        obf_ref, h_ref = rest
    else:
        (h_ref,) = rest
    j = pl.program_id(1)

    @pl.when(j == 0)
    def _():
        h_ref[...] = _rms(x_ref[...], g_ref[...]).astype(BF16)

    y = jnp.dot(h_ref[...], w_ref[...], preferred_element_type=F32)
    o_ref[...] = y
    if n_bf16:
        @pl.when(j < n_bf16)
        def _():
            obf_ref[...] = (y * jnp.where(j == 0, bf16_scale0, 1.0)).astype(BF16)


def _norm_proj(x, g, w, *, tn, n_bf16=0, bf16_scale0=1.0, tm=1024):
    m, d = x.shape
    n = w.shape[1]
    tm = _row_block(m, tm)
    parts = n // tn
    out_shape = [jax.ShapeDtypeStruct((parts, m, tn), F32)]
    out_specs = [pl.BlockSpec((None, tm, tn), lambda i, j: (j, i, 0))]
    if n_bf16:
        out_shape.append(jax.ShapeDtypeStruct((n_bf16, m, tn), BF16))
        out_specs.append(
            pl.BlockSpec((None, tm, tn), lambda i, j: (jnp.minimum(j, n_bf16 - 1), i, 0)))
    res = pl.pallas_call(
        functools.partial(_norm_proj_kernel, n_bf16=n_bf16, bf16_scale0=bf16_scale0),
        grid=(m // tm, parts),
        in_specs=[
            pl.BlockSpec((tm, d), lambda i, j: (i, 0)),
            pl.BlockSpec((1, d), lambda i, j: (0, 0)),
            pl.BlockSpec((d, tn), lambda i, j: (0, j)),
        ],
        out_specs=out_specs,
        out_shape=out_shape,
        scratch_shapes=[pltpu.VMEM((tm, d), BF16)],
        compiler_params=_params("parallel", "arbitrary"),
        name="norm_proj",
    )(x, g, w)
    return res if n_bf16 else res[0]


_BIAS_TERMS = 3


def _sb_prompt_kernel(bias_ref, q_ref, k_ref, v_ref, u2_ref, ui_ref, g_ref, o_ref, acc_ref, c_ref,
                      kaug_ref, *, bq, bk, fill_rows):
    h = pl.program_id(0)
    i = pl.program_id(1)
    bias = bias_ref[h]
    r = bq // bk
    t, dh = k_ref.shape

    @pl.when(i == 0)
    def _():
        lane = lax.broadcasted_iota(jnp.int32, (fill_rows, dh), 1)
        rem = jnp.full((fill_rows, dh), bias, F32)
        cols = jnp.zeros((fill_rows, dh), F32)
        for n in range(_BIAS_TERMS):
            term = rem.astype(BF16).astype(F32)
            cols = jnp.where(lane == n, term, cols)
            rem = rem - term
        cols = cols.astype(BF16)
        for start in range(0, t, fill_rows):
            kaug_ref[start:start + fill_rows, 0:dh] = k_ref[start:start + fill_rows, :]
            kaug_ref[start:start + fill_rows, dh:2 * dh] = cols

    ones = jnp.where(lax.broadcasted_iota(jnp.int32, (bq, dh), 1) < _BIAS_TERMS, 1.0, 0.0)
    q = jnp.concatenate([q_ref[...], ones.astype(BF16)], axis=1)
    sp0 = _softplus(jnp.full((1, bk), bias, F32))
    col_const = sp0 * (bk - lax.broadcasted_iota(jnp.int32, (1, bk), 1)).astype(F32)

    def scores(kb, q_rows):
        start = pl.multiple_of(kb * bk, bk)
        return lax.dot_general(q_rows, kaug_ref[pl.ds(start, bk), :], _NT,
                               preferred_element_type=F32), v_ref[pl.ds(start, bk), :]

    def diag_block(sub, c):
        r0 = sub * bk
        z, vblk = scores(i * r + sub, q[r0:, :])
        row = r0 + lax.broadcasted_iota(jnp.int32, (bq - r0, bk), 0)
        col = r0 + lax.broadcasted_iota(jnp.int32, (bq - r0, bk), 1)
        keep = col < row
        sp = jnp.where(keep, _softplus(z), 0.0)
        hi, lo = _split_bf16(sp)
        excl = jnp.dot(jnp.concatenate([hi, lo], axis=1), u2_ref[...], preferred_element_type=F32)
        w = jnp.where(keep, jnp.exp(z - sp - excl - c), 0.0)
        pv = jnp.dot(w.astype(BF16), vblk, preferred_element_type=F32)
        return pv, jnp.sum(sp, axis=1, keepdims=True)

    def plain_block(kb, c):
        z, vblk = scores(kb, q)
        d = (_softplus(z) - sp0).astype(BF16)
        incl = jnp.dot(d, ui_ref[...], preferred_element_type=F32)
        w = jnp.exp(z - incl - col_const - c)
        pv = jnp.dot(w.astype(BF16), vblk, preferred_element_type=F32)
        return pv, incl[:, 0:1] + col_const[:, 0:1]

    acc_ref[...] = jnp.zeros_like(acc_ref)
    c_ref[...] = jnp.zeros_like(c_ref)
    for sub in reversed(range(r)):
        r0 = sub * bk
        pv, tot = diag_block(sub, c_ref[r0:, :])
        acc_ref[r0:, :] += pv
        c_ref[r0:, :] += tot

    def body(n, carry):
        c = c_ref[...]
        pvs = None
        for d in range(r):
            pv, tot = plain_block((i - n) * r - 1 - d, c)
            pvs = pv if pvs is None else pvs + pv
            c = c + tot
        acc_ref[...] += pvs
        c_ref[...] = c
        return carry

    lax.fori_loop(0, i, body, 0)
    o_ref[...] = _rms(acc_ref[...], g_ref[...]).astype(BF16)


def _sb_prompt(qkv, bias, u_strict2, u_incl, g, *, heads, bq=1024, bk=V7X_MXU_DIM):
    _, t, w = qkv.shape
    dh = w // heads
    bq = min(bq, t)
    fill_rows = min(t, 2048)
    assert t % bq == 0 and bq % bk == 0 and t % fill_rows == 0
    return pl.pallas_call(
        functools.partial(_sb_prompt_kernel, bq=bq, bk=bk, fill_rows=fill_rows),
        grid=(heads, t // bq),
        in_specs=[
            pl.BlockSpec(memory_space=pltpu.SMEM),
            pl.BlockSpec((None, bq, dh), lambda h, i: (0, i, h)),
            pl.BlockSpec((None, t, dh), lambda h, i: (1, 0, h)),
            pl.BlockSpec((None, t, dh), lambda h, i: (2, 0, h)),
            pl.BlockSpec((2 * bk, bk), lambda h, i: (0, 0)),
            pl.BlockSpec((bk, bk), lambda h, i: (0, 0)),
            pl.BlockSpec((1, dh), lambda h, i: (0, 0)),
        ],
        out_specs=pl.BlockSpec((bq, dh), lambda h, i: (i, h)),
        out_shape=jax.ShapeDtypeStruct((t, w), BF16),
        scratch_shapes=[pltpu.VMEM((bq, dh), F32), pltpu.VMEM((bq, 1), F32),
                        pltpu.VMEM((t, 2 * dh), BF16)],
        compiler_params=_params("arbitrary", "arbitrary"),
        name="sb_prompt",
    )(bias, qkv, qkv, qkv, u_strict2, u_incl, g)


_QPAD = 8


def _sb_sample_kernel(pt_ref, q_ref, kn_ref, vn_ref, bias_ref, ut_ref, g_ref, *rest,
                      pages_per_step, heads, scale):
    kv_refs = rest[:2 * pages_per_step]
    o_ref, qrows_ref, acc_ref, c_ref, kpad_ref, vpad_ref = rest[2 * pages_per_step:]
    del pt_ref
    s = pl.program_id(1)
    ts, w = q_ref.shape
    dh = w // heads
    ncol = heads * _QPAD
    page = kpad_ref.shape[0]

    def by_key(ref):
        parts = [ref[pl.ds(hh, page, stride=heads), :] for hh in range(heads)]
        return jnp.concatenate(parts, axis=1).astype(BF16)

    def log_weights(kbs, keep=None):
        zs = [lax.dot_general(kb, qrows_ref[...], _NT, preferred_element_type=F32) * scale
              + bias_ref[...] for kb in kbs]
        sps = [_softplus(z) for z in zs]
        if keep is not None:
            sps = [jnp.where(keep, sp, 0.0) for sp in sps]
        excls = [jnp.dot(ut_ref[...], jnp.concatenate(_split_bf16(sp), axis=0),
                         preferred_element_type=F32) for sp in sps]
        return ([z - sp - excl for z, sp, excl in zip(zs, sps, excls)],
                [jnp.sum(sp, axis=0, keepdims=True) for sp in sps])

    @pl.when(s == 0)
    def _():
        q8 = jnp.concatenate([q_ref[...], jnp.zeros((_QPAD - ts, w), F32)], axis=0)
        qt = jnp.concatenate([q8] * heads, axis=0)
        row_head = lax.broadcasted_iota(jnp.int32, (ncol, w), 0) // _QPAD
        col_head = lax.broadcasted_iota(jnp.int32, (ncol, w), 1) // dh
        qrows_ref[...] = jnp.where(row_head == col_head, qt, 0.0).astype(BF16)
        kpad_ref[...] = jnp.zeros_like(kpad_ref)
        vpad_ref[...] = jnp.zeros_like(vpad_ref)
        kpad_ref[0:ts, :] = kn_ref[...]
        vpad_ref[0:ts, :] = vn_ref[...]
        key = lax.broadcasted_iota(jnp.int32, (page, ncol), 0)
        qry = lax.broadcasted_iota(jnp.int32, (page, ncol), 1) % _QPAD
        keep = (key < qry) & (qry < ts)
        (arg,), (tot,) = log_weights([kpad_ref[...].astype(BF16)], keep)
        wgt = jnp.where(keep, jnp.exp(arg), 0.0).astype(BF16)
        acc_ref[...] = lax.dot_general(wgt, vpad_ref[...].astype(BF16), _TN,
                                       preferred_element_type=F32)
        c_ref[...] = tot

    args, tots = log_weights([by_key(kv_refs[r]) for r in range(pages_per_step)])
    c = c_ref[...]
    wgts = []
    for arg, tot in zip(args, tots):
        wgts.append(jnp.exp(arg - c).astype(BF16))
        c = c + tot
    c_ref[...] = c
    vals = [by_key(kv_refs[pages_per_step + r]) for r in range(pages_per_step)]
    acc_ref[...] += lax.dot_general(jnp.concatenate(wgts, axis=0), jnp.concatenate(vals, axis=0),
                                    _TN, preferred_element_type=F32)

    @pl.when(s == pl.num_programs(1) - 1)
    def _():
        outs = []
        for hh in range(heads):
            blk = acc_ref[hh * _QPAD:hh * _QPAD + ts, hh * dh:(hh + 1) * dh]
            outs.append(_rms(blk, g_ref[...]))
        o_ref[...] = jnp.concatenate(outs, axis=1).astype(BF16)


def _sb_sample(proj, cache_k, cache_v, page_table, bias_cols, ut2, g, *, heads, pages_per_step=16):
    _, b, ts, w = proj.shape
    dh = w // heads
    page = cache_k.shape[1] // heads
    n_pages = page_table.shape[1]
    assert ts <= _QPAD and dh == V7X_LANES and cache_k.shape[2] == dh
    gp = pages_per_step if n_pages % pages_per_step == 0 else 1
    ncol = heads * _QPAD

    def page_spec(r):
        return pl.BlockSpec(
            (None, page * heads, dh),
            lambda bi, s, pt: (pt[bi, n_pages - 1 - (s * gp + r)], 0, 0))

    def tok_spec(p):
        return pl.BlockSpec((None, None, ts, w), lambda bi, s, pt: (p, bi, 0, 0))

    grid_spec = pltpu.PrefetchScalarGridSpec(
        num_scalar_prefetch=1,
        grid=(b, n_pages // gp),
        in_specs=[
            tok_spec(0), tok_spec(1), tok_spec(2),
            pl.BlockSpec((1, ncol), lambda bi, s, pt: (0, 0)),
            pl.BlockSpec((page, 2 * page), lambda bi, s, pt: (0, 0)),
            pl.BlockSpec((1, dh), lambda bi, s, pt: (0, 0)),
        ] + [page_spec(r) for r in range(gp)] * 2,
        out_specs=pl.BlockSpec((None, ts, w), lambda bi, s, pt: (bi, 0, 0)),
        scratch_shapes=[
            pltpu.VMEM((ncol, w), BF16),
            pltpu.VMEM((ncol, w), F32),
            pltpu.VMEM((1, ncol), F32),
            pltpu.VMEM((page, w), F32),
            pltpu.VMEM((page, w), F32),
        ],
    )
    return pl.pallas_call(
        functools.partial(_sb_sample_kernel, pages_per_step=gp, heads=heads, scale=dh ** -0.5),
        grid_spec=grid_spec,
        out_shape=jax.ShapeDtypeStruct((b, ts, w), BF16),
        compiler_params=_params("parallel", "arbitrary"),
        name="sb_sample",
    )(page_table, proj, proj, proj, bias_cols, ut2, g,
      *([cache_k] * gp), *([cache_v] * gp))


def _gelu_tanh(x):
    return 0.5 * x * (1.0 + jnp.tanh(math.sqrt(2.0 / math.pi) * (x + 0.044715 * (x * x * x))))


def _rg_coeffs(xc, wa_ref, ba, wx_ref, bx, lam, reset=None):
    blocks = wa_ref.shape[0]
    bw = wa_ref.shape[1]
    ra, rx = [], []
    for n in range(blocks):
        xb = xc[:, n * bw:(n + 1) * bw].astype(BF16)
        ra.append(jnp.dot(xb, wa_ref[n], preferred_element_type=F32))
        rx.append(jnp.dot(xb, wx_ref[n], preferred_element_type=F32))
    r = jax.nn.sigmoid(jnp.concatenate(ra, axis=1) + ba)
    ig = jax.nn.sigmoid(jnp.concatenate(rx, axis=1) + bx)
    log_a = -RG_C * r * _softplus(-lam)
    a = jnp.exp(log_a)
    y = 2.0 * log_a
    u = jnp.exp(y)
    mid = (u < 1.0) & (u > 0.0)
    ratio = (1.0 - u) * y / jnp.log(jnp.where(mid, u, 0.5))
    neg_expm1 = jnp.where(mid, ratio, jnp.where(u > 0.0, -y, 1.0))
    mult = jnp.sqrt(neg_expm1)
    if reset is not None:
        a = jnp.where(reset, 0.0, a)
        mult = jnp.where(reset, 1.0, mult)
    return a, mult * ig * xc


def _rglru_prompt_kernel(x_ref, gr_ref, cw_ref, cb_ref, wa_ref, ba_ref, wx_ref, bx_ref, lam_ref,
                         o_ref, hlast_ref, xbuf_ref, a_ref, b_ref, hs_ref, h_ref, *, taps):
    step = pl.program_id(0)
    tc = x_ref.shape[0]
    pad = V7X_SUBLANES

    @pl.when(step == 0)
    def _():
        xbuf_ref[0:pad, :] = jnp.zeros((pad, x_ref.shape[1]), F32)
        h_ref[...] = jnp.zeros_like(h_ref)

    xbuf_ref[pad:pad + tc, :] = x_ref[...]
    base = pad - (taps - 1)
    xc = cb_ref[...] + xbuf_ref[base:base + tc, :] * cw_ref[0:1, :]
    for tap in range(1, taps):
        xc = xc + xbuf_ref[base + tap:base + tap + tc, :] * cw_ref[tap:tap + 1, :]
    reset = (step * tc + lax.broadcasted_iota(jnp.int32, xc.shape, 0)) == 0
    a, b = _rg_coeffs(xc, wa_ref, ba_ref[...], wx_ref, bx_ref[...], lam_ref[...], reset)
    a_ref[...] = a
    b_ref[...] = b

    def body(t, h):
        h = a_ref[pl.ds(t, 1), :] * h + b_ref[pl.ds(t, 1), :]
        hs_ref[pl.ds(t, 1), :] = h
        return h

    h = lax.fori_loop(0, tc, body, h_ref[...], unroll=8)
    h_ref[...] = h
    hlast_ref[...] = h
    o_ref[...] = (_gelu_tanh(gr_ref[...]) * hs_ref[...]).astype(BF16)
    xbuf_ref[0:pad, :] = xbuf_ref[tc:tc + pad, :]


def _rglru_prompt(proj, conv_w, conv_b, wa, ba, wx, bx, lam, *, x_part, gate_part, tc=512):
    _, t, w = proj.shape
    taps = conv_w.shape[0]
    tc = _row_block(t, tc)
    assert tc % V7X_SUBLANES == 0 and taps - 1 <= V7X_SUBLANES
    const2 = lambda i: (0, 0)
    const3 = lambda i: (0, 0, 0)
    return pl.pallas_call(
        functools.partial(_rglru_prompt_kernel, taps=taps),
        grid=(t // tc,),
        in_specs=[
            pl.BlockSpec((None, tc, w), lambda i: (x_part, i, 0)),
            pl.BlockSpec((None, tc, w), lambda i: (gate_part, i, 0)),
            pl.BlockSpec(conv_w.shape, const2),
            pl.BlockSpec((1, w), const2),
            pl.BlockSpec(wa.shape, const3),
            pl.BlockSpec((1, w), const2),
            pl.BlockSpec(wx.shape, const3),
            pl.BlockSpec((1, w), const2),
            pl.BlockSpec((1, w), const2),
        ],
        out_specs=[pl.BlockSpec((tc, w), lambda i: (i, 0)), pl.BlockSpec((1, w), const2)],
        out_shape=[jax.ShapeDtypeStruct((t, w), BF16), jax.ShapeDtypeStruct((1, w), F32)],
        scratch_shapes=[
            pltpu.VMEM((tc + V7X_SUBLANES, w), F32),
            pltpu.VMEM((tc, w), F32),
            pltpu.VMEM((tc, w), F32),
            pltpu.VMEM((tc, w), F32),
            pltpu.VMEM((1, w), F32),
        ],
        compiler_params=_params("arbitrary"),
        name="rglru_prompt",
    )(proj, proj, conv_w, conv_b, wa, ba, wx, bx, lam)


def _rglru_sample_kernel(x_ref, buf_ref, gr_ref, h0_ref, cw_ref, cb_ref, wa_ref, ba_ref, wx_ref,
                         bx_ref, lam_ref, o_ref, hnew_ref, *, first_pos):
    ts = x_ref.shape[0]
    taps = cw_ref.shape[0]
    rows = [buf_ref[n] for n in range(taps - 1)] + [x_ref[n] for n in range(ts)]
    h = h0_ref[...]
    for t in range(ts):
        xc = cb_ref[...] + rows[t] * cw_ref[0:1, :]
        for tap in range(1, taps):
            xc = xc + rows[t + tap] * cw_ref[tap:tap + 1, :]
        reset = jnp.full(xc.shape, True) if first_pos + t == 0 else None
        a, b = _rg_coeffs(xc, wa_ref, ba_ref[...], wx_ref, bx_ref[...], lam_ref[...], reset)
        h = a * h + b
        o_ref[t] = (_gelu_tanh(gr_ref[t]) * h).astype(BF16)
    hnew_ref[...] = h


def _rglru_sample(x_tm, buf_tm, gr_tm, h0, conv_w, conv_b, wa, ba, wx, bx, lam, *, first_pos):
    ts, b, w = x_tm.shape
    return pl.pallas_call(
        functools.partial(_rglru_sample_kernel, first_pos=first_pos),
        out_shape=[jax.ShapeDtypeStruct((ts, b, w), BF16), jax.ShapeDtypeStruct((b, w), F32)],
        compiler_params=pltpu.CompilerParams(vmem_limit_bytes=VMEM_LIMIT),
        name="rglru_sample",
    )(x_tm, buf_tm, gr_tm, h0, conv_w, conv_b, wa, ba, wx, bx, lam)


def _mem_attn_kernel(q_ref, mk_ref, mv_ref, o_ref, *, heads):
    w = q_ref.shape[-1]
    dh = w // heads
    outs = []
    for hh in range(heads):
        sl = slice(hh * dh, (hh + 1) * dh)
        qh = q_ref[:, sl].astype(BF16)
        kh = mk_ref[:, sl].astype(BF16)
        vh = mv_ref[:, sl].astype(BF16)
        s = lax.dot_general(qh, kh, _NT, preferred_element_type=F32) * (dh ** -0.5)
        e = jnp.exp(s - jnp.max(s, axis=-1, keepdims=True))
        p = e / jnp.sum(e, axis=-1, keepdims=True)
        outs.append(jnp.dot(p.astype(BF16), vh, preferred_element_type=F32))
    o_ref[...] = jnp.concatenate(outs, axis=1).astype(BF16)


def _mem_attn_prompt(proj, memkv, *, q_part, heads, tm=512):
    _, t, w = proj.shape
    mtok = memkv.shape[1]
    tm = _row_block(t, tm)
    return pl.pallas_call(
        functools.partial(_mem_attn_kernel, heads=heads),
        grid=(t // tm,),
        in_specs=[
            pl.BlockSpec((None, tm, w), lambda i: (q_part, i, 0)),
            pl.BlockSpec((None, mtok, w), lambda i: (0, 0, 0)),
            pl.BlockSpec((None, mtok, w), lambda i: (1, 0, 0)),
        ],
        out_specs=pl.BlockSpec((tm, w), lambda i: (i, 0)),
        out_shape=jax.ShapeDtypeStruct((t, w), BF16),
        compiler_params=_params("parallel"),
        name="mem_attn_prompt",
    )(proj, memkv, memkv)


def _mem_attn_sample(proj, mem_k, mem_v, *, q_part, heads):
    _, b, ts, w = proj.shape
    mtok = mem_k.shape[1]
    return pl.pallas_call(
        functools.partial(_mem_attn_kernel, heads=heads),
        grid=(b,),
        in_specs=[
            pl.BlockSpec((None, None, ts, w), lambda i: (q_part, i, 0, 0)),
            pl.BlockSpec((None, mtok, w), lambda i: (i, 0, 0)),
            pl.BlockSpec((None, mtok, w), lambda i: (i, 0, 0)),
        ],
        out_specs=pl.BlockSpec((None, ts, w), lambda i: (i, 0, 0)),
        out_shape=jax.ShapeDtypeStruct((b, ts, w), BF16),
        compiler_params=_params("parallel"),
        name="mem_attn_sample",
    )(proj, mem_k, mem_v)


def _merge_kernel(*refs, n_branch, gate_parts):
    o_refs = refs[:n_branch]
    g_refs = refs[n_branch:n_branch + n_branch * gate_parts]
    bg_ref, wb_ref, m_ref = refs[n_branch + n_branch * gate_parts:]
    w = g_refs[0].shape[1]
    for part in range(gate_parts):
        cols = slice(part * w, (part + 1) * w)
        acc = None
        for n in range(n_branch):
            gate = jax.nn.sigmoid(g_refs[n * gate_parts + part][...] + bg_ref[n][:, cols])
            term = gate * jnp.dot(o_refs[n][...], wb_ref[n, :, cols], preferred_element_type=F32)
            acc = term if acc is None else acc + term
        m_ref[:, cols] = acc.astype(BF16)


def _merge(branches, proj, b_gate3, w_branch, *, gate_part0, tm=256):
    m, w = branches[0].shape
    n_branch, _, d = w_branch.shape
    tm = _row_block(m, tm)
    gate_parts = d // w

    def gate_spec(p):
        return pl.BlockSpec((None, tm, w), lambda i: (gate_part0 + p, i, 0))

    resident = dict(pipeline_mode=pl.Buffered(1))
    return pl.pallas_call(
        functools.partial(_merge_kernel, n_branch=n_branch, gate_parts=gate_parts),
        grid=(m // tm,),
        in_specs=[pl.BlockSpec((tm, w), lambda i: (i, 0))] * n_branch
        + [gate_spec(p) for p in range(n_branch * gate_parts)]
        + [pl.BlockSpec((n_branch, 1, d), lambda i: (0, 0, 0), **resident),
           pl.BlockSpec((n_branch, w, d), lambda i: (0, 0, 0), **resident)],
        out_specs=pl.BlockSpec((tm, d), lambda i: (i, 0)),
        out_shape=jax.ShapeDtypeStruct((m, d), BF16),
        compiler_params=_params("parallel"),
        name="merge",
    )(*branches, *([proj] * (n_branch * gate_parts)), b_gate3, w_branch)


def _out_proj_kernel(x_ref, m_ref, w_ref, g_ref, o_ref):
    y = jnp.dot(m_ref[...], w_ref[...], preferred_element_type=F32)
    o_ref[...] = x_ref[...] + _rms(y, g_ref[...])


def _out_proj(x, merged, w_out, g, *, tm=512):
    m, d = x.shape
    tm = _row_block(m, tm)
    return pl.pallas_call(
        _out_proj_kernel,
        grid=(m // tm,),
        in_specs=[
            pl.BlockSpec((tm, d), lambda i: (i, 0)),
            pl.BlockSpec((tm, d), lambda i: (i, 0)),
            pl.BlockSpec((d, d), lambda i: (0, 0)),
            pl.BlockSpec((1, d), lambda i: (0, 0)),
        ],
        out_specs=pl.BlockSpec((tm, d), lambda i: (i, 0)),
        out_shape=jax.ShapeDtypeStruct((m, d), F32),
        compiler_params=_params("parallel"),
        name="out_proj",
    )(x, merged, w_out, g)


def _strict_upper_pair(n):
    j = lax.broadcasted_iota(jnp.int32, (n, n), 0)
    s = lax.broadcasted_iota(jnp.int32, (n, n), 1)
    u = (j > s).astype(BF16)
    return jnp.concatenate([u, u], axis=0)


def kernel(x_prompt, x_sample, cache_sb_k, cache_sb_v, state_conv, state_rglru, cache_mem_k, cache_mem_v, page_table, mem_prompt, ffn1_g_pre, ffn1_g_post, ffn1_w_gu, ffn1_w_down, mix_g_pre, mix_g_post, w_in, b_gate, sb_bias, sb_norm_g, conv_w, conv_b, rg_w_a, rg_b_a, rg_w_x, rg_b_x, rg_lambda, mem_g, w_mem_kv, w_branch, w_out, ffn2_g_pre, ffn2_g_post, ffn2_w_gu, ffn2_w_down):
    bp, tp, d = x_prompt.shape
    bs, ts, _ = x_sample.shape
    assert bp == 1
    n_pool, page, sb_heads, sb_dh = cache_sb_k.shape
    mix_w = sb_heads * sb_dh
    n_pages = page_table.shape[1]
    past_len = n_pages * page
    mem_tok, mem_heads = cache_mem_k.shape[1], cache_mem_k.shape[2]

    row = lambda v: v.reshape(1, -1).astype(F32)
    bf = lambda v: v.astype(BF16)
    w1_gu, w1_down = bf(ffn1_w_gu), bf(ffn1_w_down)
    w2_gu, w2_down = bf(ffn2_w_gu), bf(ffn2_w_down)
    w_in_b, w_mem_b = bf(w_in), bf(w_mem_kv)
    w_branch_b, w_out_b = bf(w_branch), bf(w_out)
    wa_b, wx_b = bf(rg_w_a), bf(rg_w_x)
    b_gate3 = b_gate.reshape(N_BRANCH, 1, d)
    sb_g = row(sb_norm_g)
    rg_vecs = (conv_w, row(conv_b), wa_b, row(rg_b_a), wx_b, row(rg_b_x), row(rg_lambda))
    P_Q, P_K, P_V, P_X, P_GR, P_QM, P_GL = range(7)

    def ffn(x, n):
        if n == 1:
            return _ffn(x, row(ffn1_g_pre), row(ffn1_g_post), w1_gu, w1_down)
        return _ffn(x, row(ffn2_g_pre), row(ffn2_g_post), w2_gu, w2_down)

    def finish(x, branches, proj):
        merged = _merge(branches, proj, b_gate3, w_branch_b, gate_part0=P_GL)
        x = _out_proj(x, merged, w_out_b, row(mix_g_post))
        return ffn(x, 2)

    memkv = _norm_proj(mem_prompt.reshape(mem_tok, d), row(mem_g), w_mem_b, tn=mix_w)
    xp = ffn(x_prompt.reshape(tp, d), 1)
    proj_p, qkv_p = _norm_proj(xp, row(mix_g_pre), w_in_b, tn=mix_w, n_bf16=3,
                               bf16_scale0=sb_dh ** -0.5)
    u_strict2 = _strict_upper_pair(V7X_MXU_DIM)
    u_incl = u_strict2[:V7X_MXU_DIM] + jnp.eye(V7X_MXU_DIM, dtype=BF16)
    o_sb_p = _sb_prompt(qkv_p, sb_bias.astype(F32), u_strict2, u_incl, sb_g, heads=sb_heads)
    o_rg_p, h_p = _rglru_prompt(proj_p, *rg_vecs, x_part=P_X, gate_part=P_GR)
    o_mem_p = _mem_attn_prompt(proj_p, memkv, q_part=P_QM, heads=mem_heads)
    y_p = finish(xp, (o_sb_p, o_rg_p, o_mem_p), proj_p)

    ms = bs * ts
    xs = ffn(x_sample.reshape(ms, d), 1)
    proj_s = _norm_proj(xs, row(mix_g_pre), w_in_b, tn=mix_w)
    proj_s4 = proj_s.reshape(proj_s.shape[0], bs, ts, mix_w)
    ut = _strict_upper_pair(page)
    ut2 = jnp.concatenate([ut[:page].T, ut[:page].T], axis=1)
    bias_cols = jnp.repeat(sb_bias.astype(F32), _QPAD).reshape(1, sb_heads * _QPAD)
    o_sb_s = _sb_sample(proj_s4, cache_sb_k.reshape(n_pool, page * sb_heads, sb_dh),
                        cache_sb_v.reshape(n_pool, page * sb_heads, sb_dh), page_table, bias_cols,
                        ut2, sb_g, heads=sb_heads)
    to_tm = lambda a: a.reshape(bs, ts, mix_w).transpose(1, 0, 2)
    o_rg_tm, h_s = _rglru_sample(to_tm(proj_s[P_X]), state_conv.transpose(1, 0, 2),
                                 to_tm(proj_s[P_GR]), state_rglru, *rg_vecs, first_pos=past_len)
    o_rg_s = o_rg_tm.transpose(1, 0, 2)
    o_mem_s = _mem_attn_sample(proj_s4, cache_mem_k.reshape(bs, mem_tok, mix_w),
                               cache_mem_v.reshape(bs, mem_tok, mix_w), q_part=P_QM,
                               heads=mem_heads)
    branches_s = tuple(o.reshape(ms, mix_w) for o in (o_sb_s, o_rg_s, o_mem_s))
    y_s = finish(xs, branches_s, proj_s)

    n_keep = conv_w.shape[0] - 1
    xr_s = proj_s[P_X].reshape(bs, ts, mix_w)
    conv_s = jnp.concatenate([state_conv.astype(F32), xr_s], axis=1)[:, ts:]
    conv_p = lax.slice(proj_p, (P_X, tp - n_keep, 0), (P_X + 1, tp, mix_w))
    heads4 = lambda a, b_, t_: a.reshape(b_, t_, sb_heads, sb_dh)
    return (
        y_p.reshape(bp, tp, d),
        y_s.reshape(bs, ts, d),
        heads4(proj_p[P_K], bp, tp), heads4(proj_p[P_V], bp, tp),
        heads4(proj_s[P_K], bs, ts), heads4(proj_s[P_V], bs, ts),
        conv_p, conv_s,
        h_p, h_s,
        memkv[0].reshape(bp, mem_tok, mem_heads, mix_w // mem_heads),
        memkv[1].reshape(bp, mem_tok, mem_heads, mix_w // mem_heads),
    )
```

```python
import functools
import math

import jax
import jax.numpy as jnp
from jax import lax
from jax.experimental import pallas as pl
from jax.experimental.pallas import tpu as pltpu

F32 = jnp.float32
BF16 = jnp.bfloat16

EPS = 1e-6
RG_C = 8.0
N_BRANCH = 3
SB_HEADS = 8
MEM_HEADS = 4
RG_BLOCKS = 8

V7X_LANES = 128
V7X_SUBLANES = 8
V7X_MXU_DIM = 256
V7X_VMEM_BYTES = 64 * 1024 * 1024
VMEM_LIMIT = V7X_VMEM_BYTES - 8 * 1024 * 1024

_LOG2E = 1.4426950408889634
_NT = (((1,), (1,)), ((), ()))
_TN = (((0,), (0,)), ((), ()))


def _params(*sem):
    return pltpu.CompilerParams(dimension_semantics=sem, vmem_limit_bytes=VMEM_LIMIT)


def _rms(x, g):
    ms = jnp.mean(x * x, axis=-1, keepdims=True)
    return x * lax.rsqrt(ms + EPS) * g


def _softplus(z):
    return jnp.maximum(z, 0.0) + jnp.log(1.0 + jnp.exp2(jnp.abs(z) * (-_LOG2E)))


def _split_bf16(x):
    hi = x.astype(BF16)
    lo = (x - hi.astype(F32)).astype(BF16)
    return hi, lo


def _row_block(m, target):
    return target if m % target == 0 else m


def _ffn_kernel(x_ref, gpre_ref, gpost_ref, wg_ref, wu_ref, wd_ref, o_ref, h_ref):
    j = pl.program_id(1)

    @pl.when(j == 0)
    def _():
        h_ref[...] = _rms(x_ref[...], gpre_ref[...]).astype(BF16)
        o_ref[...] = jnp.zeros_like(o_ref)

    h = h_ref[...]
    gate = jnp.dot(h, wg_ref[...], preferred_element_type=F32)
    up = jnp.dot(h, wu_ref[...], preferred_element_type=F32)
    act = (gate * jax.nn.sigmoid(gate) * up).astype(BF16)
    o_ref[...] += jnp.dot(act, wd_ref[...], preferred_element_type=F32)

    @pl.when(j == pl.num_programs(1) - 1)
    def _():
        o_ref[...] = x_ref[...] + 0.5 * _rms(o_ref[...], gpost_ref[...])


def _ffn(x, g_pre, g_post, w_gu, w_down, *, tm=512, tf=512):
    m, d = x.shape
    f = w_down.shape[0]
    tm = _row_block(m, tm)
    nf = f // tf
    return pl.pallas_call(
        _ffn_kernel,
        grid=(m // tm, nf),
        in_specs=[
            pl.BlockSpec((tm, d), lambda i, j: (i, 0)),
            pl.BlockSpec((1, d), lambda i, j: (0, 0)),
            pl.BlockSpec((1, d), lambda i, j: (0, 0)),
            pl.BlockSpec((d, tf), lambda i, j: (0, j)),
            pl.BlockSpec((d, tf), lambda i, j: (0, j + nf)),
            pl.BlockSpec((tf, d), lambda i, j: (j, 0)),
        ],
        out_specs=pl.BlockSpec((tm, d), lambda i, j: (i, 0)),
        out_shape=jax.ShapeDtypeStruct((m, d), F32),
        scratch_shapes=[pltpu.VMEM((tm, d), BF16)],
        compiler_params=_params("parallel", "arbitrary"),
        name="ffn",
    )(x, g_pre, g_post, w_gu, w_gu, w_down)


def _norm_proj_kernel(x_ref, g_ref, w_ref, o_ref, *rest, n_bf16, bf16_scale0):
    if n_bf16:
        obf_ref, h_ref = rest
    else:
        (h_ref,) = rest
    j = pl.program_id(1)

    @pl.when(j == 0)
    def _():
        h_ref[...] = _rms(x_ref[...], g_ref[...]).astype(BF16)

    y = jnp.dot(h_ref[...], w_ref[...], preferred_element_type=F32)
    o_ref[...] = y
    if n_bf16:
        @pl.when(j < n_bf16)
        def _():
            obf_ref[...] = (y * jnp.where(j == 0, bf16_scale0, 1.0)).astype(BF16)


def _norm_proj(x, g, w, *, tn, n_bf16=0, bf16_scale0=1.0, tm=1024):
    m, d = x.shape
    n = w.shape[1]
    tm = _row_block(m, tm)
    parts = n // tn
    out_shape = [jax.ShapeDtypeStruct((parts, m, tn), F32)]
    out_specs = [pl.BlockSpec((None, tm, tn), lambda i, j: (j, i, 0))]
    if n_bf16:
        out_shape.append(jax.ShapeDtypeStruct((n_bf16, m, tn), BF16))
        out_specs.append(
            pl.BlockSpec((None, tm, tn), lambda i, j: (jnp.minimum(j, n_bf16 - 1), i, 0)))
    res = pl.pallas_call(
        functools.partial(_norm_proj_kernel, n_bf16=n_bf16, bf16_scale0=bf16_scale0),
        grid=(m // tm, parts),
        in_specs=[
            pl.BlockSpec((tm, d), lambda i, j: (i, 0)),
            pl.BlockSpec((1, d), lambda i, j: (0, 0)),
            pl.BlockSpec((d, tn), lambda i, j: (0, j)),
        ],
        out_specs=out_specs,
        out_shape=out_shape,
        scratch_shapes=[pltpu.VMEM((tm, d), BF16)],
        compiler_params=_params("parallel", "arbitrary"),
        name="norm_proj",
    )(x, g, w)
    return res if n_bf16 else res[0]


_BIAS_TERMS = 3


def _sb_prompt_kernel(bias_ref, q_ref, k_ref, v_ref, u2_ref, ui_ref, g_ref, o_ref, acc_ref, c_ref,
                      kaug_ref, *, bq, bk, fill_rows):
    h = pl.program_id(0)
    i = pl.program_id(1)
    bias = bias_ref[h]
    r = bq // bk
    t, dh = k_ref.shape

    @pl.when(i == 0)
    def _():
        lane = lax.broadcasted_iota(jnp.int32, (fill_rows, dh), 1)
        rem = jnp.full((fill_rows, dh), bias, F32)
        cols = jnp.zeros((fill_rows, dh), F32)
        for n in range(_BIAS_TERMS):
            term = rem.astype(BF16).astype(F32)
            cols = jnp.where(lane == n, term, cols)
            rem = rem - term
        cols = cols.astype(BF16)
        for start in range(0, t, fill_rows):
            kaug_ref[start:start + fill_rows, 0:dh] = k_ref[start:start + fill_rows, :]
            kaug_ref[start:start + fill_rows, dh:2 * dh] = cols

    ones = jnp.where(lax.broadcasted_iota(jnp.int32, (bq, dh), 1) < _BIAS_TERMS, 1.0, 0.0)
    q = jnp.concatenate([q_ref[...], ones.astype(BF16)], axis=1)
    sp0 = _softplus(jnp.full((1, bk), bias, F32))
    col_const = sp0 * (bk - lax.broadcasted_iota(jnp.int32, (1, bk), 1)).astype(F32)

    def scores(kb, q_rows):
        start = pl.multiple_of(kb * bk, bk)
        return lax.dot_general(q_rows, kaug_ref[pl.ds(start, bk), :], _NT,
                               preferred_element_type=F32), v_ref[pl.ds(start, bk), :]

    def diag_tile(kb, q_rows, c):
        z, vblk = scores(kb, q_rows)
        keep = (lax.broadcasted_iota(jnp.int32, (bk, bk), 1)
                < lax.broadcasted_iota(jnp.int32, (bk, bk), 0))
        sp = jnp.where(keep, _softplus(z), 0.0)
        hi, lo = _split_bf16(sp)
        excl = jnp.dot(jnp.concatenate([hi, lo], axis=1), u2_ref[...], preferred_element_type=F32)
        w = jnp.where(keep, jnp.exp(z - sp - excl - c), 0.0)
        pv = jnp.dot(w.astype(BF16), vblk, preferred_element_type=F32)
        return pv, jnp.sum(sp, axis=1, keepdims=True)

    def plain_block(kb, q_rows, c):
        z, vblk = scores(kb, q_rows)
        d = (_softplus(z) - sp0).astype(BF16)
        incl = jnp.dot(d, ui_ref[...], preferred_element_type=F32)
        w = jnp.exp(z - incl - col_const - c)
        pv = jnp.dot(w.astype(BF16), vblk, preferred_element_type=F32)
        return pv, incl[:, 0:1] + col_const[:, 0:1]

    acc_ref[...] = jnp.zeros_like(acc_ref)
    c_ref[...] = jnp.zeros_like(c_ref)
    for sub in reversed(range(r)):
        r0, r1 = sub * bk, (sub + 1) * bk
        pv, tot = diag_tile(i * r + sub, q[r0:r1, :], c_ref[r0:r1, :])
        acc_ref[r0:r1, :] += pv
        c_ref[r0:r1, :] += tot
        if r1 < bq:
            pv, tot = plain_block(i * r + sub, q[r1:, :], c_ref[r1:, :])
            acc_ref[r1:, :] += pv
            c_ref[r1:, :] += tot

    def body(n, carry):
        c = c_ref[...]
        pvs = None
        for d in range(r):
            pv, tot = plain_block((i - n) * r - 1 - d, q, c)
            pvs = pv if pvs is None else pvs + pv
            c = c + tot
        acc_ref[...] += pvs
        c_ref[...] = c
        return carry

    lax.fori_loop(0, i, body, 0)
    o_ref[...] = _rms(acc_ref[...], g_ref[...]).astype(BF16)


def _sb_prompt(qkv, bias, u_strict2, u_incl, g, *, heads, bq=2048, bk=V7X_MXU_DIM):
    _, t, w = qkv.shape
    dh = w // heads
    bq = min(bq, t)
    fill_rows = min(t, 2048)
    assert t % bq == 0 and bq % bk == 0 and t % fill_rows == 0
    return pl.pallas_call(
        functools.partial(_sb_prompt_kernel, bq=bq, bk=bk, fill_rows=fill_rows),
        grid=(heads, t // bq),
        in_specs=[
            pl.BlockSpec(memory_space=pltpu.SMEM),
            pl.BlockSpec((None, bq, dh), lambda h, i: (0, i, h)),
            pl.BlockSpec((None, t, dh), lambda h, i: (1, 0, h)),
            pl.BlockSpec((None, t, dh), lambda h, i: (2, 0, h)),
            pl.BlockSpec((2 * bk, bk), lambda h, i: (0, 0)),
            pl.BlockSpec((bk, bk), lambda h, i: (0, 0)),
            pl.BlockSpec((1, dh), lambda h, i: (0, 0)),
        ],
        out_specs=pl.BlockSpec((bq, dh), lambda h, i: (i, h)),
        out_shape=jax.ShapeDtypeStruct((t, w), BF16),
        scratch_shapes=[pltpu.VMEM((bq, dh), F32), pltpu.VMEM((bq, 1), F32),
                        pltpu.VMEM((t, 2 * dh), BF16)],
        compiler_params=_params("arbitrary", "arbitrary"),
        name="sb_prompt",
    )(bias, qkv, qkv, qkv, u_strict2, u_incl, g)


_QPAD = 8


def _sb_sample_kernel(pt_ref, q_ref, kn_ref, vn_ref, bias_ref, ut_ref, g_ref, *rest,
                      pages_per_step, heads, scale):
    kv_refs = rest[:2 * pages_per_step]
    o_ref, qrows_ref, acc_ref, c_ref, kpad_ref, vpad_ref = rest[2 * pages_per_step:]
    del pt_ref
    s = pl.program_id(1)
    ts, w = q_ref.shape
    dh = w // heads
    ncol = heads * _QPAD
    page = kpad_ref.shape[0]

    def by_key(ref):
        parts = [ref[pl.ds(hh, page, stride=heads), :] for hh in range(heads)]
        return jnp.concatenate(parts, axis=1).astype(BF16)

    def log_weights(kbs, keep=None):
        zs = [lax.dot_general(kb, qrows_ref[...], _NT, preferred_element_type=F32) * scale
              + bias_ref[...] for kb in kbs]
        sps = [_softplus(z) for z in zs]
        if keep is not None:
            sps = [jnp.where(keep, sp, 0.0) for sp in sps]
        excls = [jnp.dot(ut_ref[...], jnp.concatenate(_split_bf16(sp), axis=0),
                         preferred_element_type=F32) for sp in sps]
        return ([z - sp - excl for z, sp, excl in zip(zs, sps, excls)],
                [jnp.sum(sp, axis=0, keepdims=True) for sp in sps])

    @pl.when(s == 0)
    def _():
        q8 = jnp.concatenate([q_ref[...], jnp.zeros((_QPAD - ts, w), F32)], axis=0)
        qt = jnp.concatenate([q8] * heads, axis=0)
        row_head = lax.broadcasted_iota(jnp.int32, (ncol, w), 0) // _QPAD
        col_head = lax.broadcasted_iota(jnp.int32, (ncol, w), 1) // dh
        qrows_ref[...] = jnp.where(row_head == col_head, qt, 0.0).astype(BF16)
        kpad_ref[...] = jnp.zeros_like(kpad_ref)
        vpad_ref[...] = jnp.zeros_like(vpad_ref)
        kpad_ref[0:ts, :] = kn_ref[...]
        vpad_ref[0:ts, :] = vn_ref[...]
        key = lax.broadcasted_iota(jnp.int32, (page, ncol), 0)
        qry = lax.broadcasted_iota(jnp.int32, (page, ncol), 1) % _QPAD
        keep = (key < qry) & (qry < ts)
        (arg,), (tot,) = log_weights([kpad_ref[...].astype(BF16)], keep)
        wgt = jnp.where(keep, jnp.exp(arg), 0.0).astype(BF16)
        acc_ref[...] = lax.dot_general(wgt, vpad_ref[...].astype(BF16), _TN,
                                       preferred_element_type=F32)
        c_ref[...] = tot

    args, tots = log_weights([by_key(kv_refs[r]) for r in range(pages_per_step)])
    c = c_ref[...]
    wgts = []
    for arg, tot in zip(args, tots):
        wgts.append(jnp.exp(arg - c).astype(BF16))
        c = c + tot
    c_ref[...] = c
    vals = [by_key(kv_refs[pages_per_step + r]) for r in range(pages_per_step)]
    acc_ref[...] += lax.dot_general(jnp.concatenate(wgts, axis=0), jnp.concatenate(vals, axis=0),
                                    _TN, preferred_element_type=F32)

    @pl.when(s == pl.num_programs(1) - 1)
    def _():
        outs = []
        for hh in range(heads):
            blk = acc_ref[hh * _QPAD:hh * _QPAD + ts, hh * dh:(hh + 1) * dh]
            outs.append(_rms(blk, g_ref[...]))
        o_ref[...] = jnp.concatenate(outs, axis=1).astype(BF16)


def _sb_sample(proj, cache_k, cache_v, page_table, bias_cols, ut2, g, *, heads, pages_per_step=16):
    _, b, ts, w = proj.shape
    dh = w // heads
    page = cache_k.shape[1] // heads
    n_pages = page_table.shape[1]
    assert ts <= _QPAD and dh == V7X_LANES and cache_k.shape[2] == dh
    gp = pages_per_step if n_pages % pages_per_step == 0 else 1
    ncol = heads * _QPAD

    def page_spec(r):
        return pl.BlockSpec(
            (None, page * heads, dh),
            lambda bi, s, pt: (pt[bi, n_pages - 1 - (s * gp + r)], 0, 0))

    def tok_spec(p):
        return pl.BlockSpec((None, None, ts, w), lambda bi, s, pt: (p, bi, 0, 0))

    grid_spec = pltpu.PrefetchScalarGridSpec(
        num_scalar_prefetch=1,
        grid=(b, n_pages // gp),
        in_specs=[
            tok_spec(0), tok_spec(1), tok_spec(2),
            pl.BlockSpec((1, ncol), lambda bi, s, pt: (0, 0)),
            pl.BlockSpec((page, 2 * page), lambda bi, s, pt: (0, 0)),
            pl.BlockSpec((1, dh), lambda bi, s, pt: (0, 0)),
        ] + [page_spec(r) for r in range(gp)] * 2,
        out_specs=pl.BlockSpec((None, ts, w), lambda bi, s, pt: (bi, 0, 0)),
        scratch_shapes=[
            pltpu.VMEM((ncol, w), BF16),
            pltpu.VMEM((ncol, w), F32),
            pltpu.VMEM((1, ncol), F32),
            pltpu.VMEM((page, w), F32),
            pltpu.VMEM((page, w), F32),
        ],
    )
    return pl.pallas_call(
        functools.partial(_sb_sample_kernel, pages_per_step=gp, heads=heads, scale=dh ** -0.5),
        grid_spec=grid_spec,
        out_shape=jax.ShapeDtypeStruct((b, ts, w), BF16),
        compiler_params=_params("parallel", "arbitrary"),
        name="sb_sample",
    )(page_table, proj, proj, proj, bias_cols, ut2, g,
      *([cache_k] * gp), *([cache_v] * gp))


def _gelu_tanh(x):
    return 0.5 * x * (1.0 + jnp.tanh(math.sqrt(2.0 / math.pi) * (x + 0.044715 * (x * x * x))))


def _rg_coeffs(xc, wa_ref, ba, wx_ref, bx, lam, reset=None):
    blocks = wa_ref.shape[0]
    bw = wa_ref.shape[1]
    ra, rx = [], []
    for n in range(blocks):
        xb = xc[:, n * bw:(n + 1) * bw].astype(BF16)
        ra.append(jnp.dot(xb, wa_ref[n], preferred_element_type=F32))
        rx.append(jnp.dot(xb, wx_ref[n], preferred_element_type=F32))
    r = jax.nn.sigmoid(jnp.concatenate(ra, axis=1) + ba)
    ig = jax.nn.sigmoid(jnp.concatenate(rx, axis=1) + bx)
    log_a = -RG_C * r * _softplus(-lam)
    a = jnp.exp(log_a)
    y = 2.0 * log_a
    u = jnp.exp(y)
    mid = (u < 1.0) & (u > 0.0)
    ratio = (1.0 - u) * y / jnp.log(jnp.where(mid, u, 0.5))
    neg_expm1 = jnp.where(mid, ratio, jnp.where(u > 0.0, -y, 1.0))
    mult = jnp.sqrt(neg_expm1)
    if reset is not None:
        a = jnp.where(reset, 0.0, a)
        mult = jnp.where(reset, 1.0, mult)
    return a, mult * ig * xc


def _rglru_prompt_kernel(x_ref, gr_ref, cw_ref, cb_ref, wa_ref, ba_ref, wx_ref, bx_ref, lam_ref,
                         o_ref, hlast_ref, xbuf_ref, a_ref, b_ref, hs_ref, h_ref, *, taps):
    step = pl.program_id(0)
    tc = x_ref.shape[0]
    pad = V7X_SUBLANES

    @pl.when(step == 0)
    def _():
        xbuf_ref[0:pad, :] = jnp.zeros((pad, x_ref.shape[1]), F32)
        h_ref[...] = jnp.zeros_like(h_ref)

    xbuf_ref[pad:pad + tc, :] = x_ref[...]
    base = pad - (taps - 1)
    xc = cb_ref[...] + xbuf_ref[base:base + tc, :] * cw_ref[0:1, :]
    for tap in range(1, taps):
        xc = xc + xbuf_ref[base + tap:base + tap + tc, :] * cw_ref[tap:tap + 1, :]
    reset = (step * tc + lax.broadcasted_iota(jnp.int32, xc.shape, 0)) == 0
    a, b = _rg_coeffs(xc, wa_ref, ba_ref[...], wx_ref, bx_ref[...], lam_ref[...], reset)
    a_ref[...] = a
    b_ref[...] = b

    def body(t, h):
        h = a_ref[pl.ds(t, 1), :] * h + b_ref[pl.ds(t, 1), :]
        hs_ref[pl.ds(t, 1), :] = h
        return h

    h = lax.fori_loop(0, tc, body, h_ref[...], unroll=8)
    h_ref[...] = h
    hlast_ref[...] = h
    o_ref[...] = (_gelu_tanh(gr_ref[...]) * hs_ref[...]).astype(BF16)
    xbuf_ref[0:pad, :] = xbuf_ref[tc:tc + pad, :]


def _rglru_prompt(proj, conv_w, conv_b, wa, ba, wx, bx, lam, *, x_part, gate_part, tc=512):
    _, t, w = proj.shape
    taps = conv_w.shape[0]
    tc = _row_block(t, tc)
    assert tc % V7X_SUBLANES == 0 and taps - 1 <= V7X_SUBLANES
    const2 = lambda i: (0, 0)
    const3 = lambda i: (0, 0, 0)
    return pl.pallas_call(
        functools.partial(_rglru_prompt_kernel, taps=taps),
        grid=(t // tc,),
        in_specs=[
            pl.BlockSpec((None, tc, w), lambda i: (x_part, i, 0)),
            pl.BlockSpec((None, tc, w), lambda i: (gate_part, i, 0)),
            pl.BlockSpec(conv_w.shape, const2),
            pl.BlockSpec((1, w), const2),
            pl.BlockSpec(wa.shape, const3),
            pl.BlockSpec((1, w), const2),
            pl.BlockSpec(wx.shape, const3),
            pl.BlockSpec((1, w), const2),
            pl.BlockSpec((1, w), const2),
        ],
        out_specs=[pl.BlockSpec((tc, w), lambda i: (i, 0)), pl.BlockSpec((1, w), const2)],
        out_shape=[jax.ShapeDtypeStruct((t, w), BF16), jax.ShapeDtypeStruct((1, w), F32)],
        scratch_shapes=[
            pltpu.VMEM((tc + V7X_SUBLANES, w), F32),
            pltpu.VMEM((tc, w), F32),
            pltpu.VMEM((tc, w), F32),
            pltpu.VMEM((tc, w), F32),
            pltpu.VMEM((1, w), F32),
        ],
        compiler_params=_params("arbitrary"),
        name="rglru_prompt",
    )(proj, proj, conv_w, conv_b, wa, ba, wx, bx, lam)


def _rglru_sample_kernel(x_ref, buf_ref, gr_ref, h0_ref, cw_ref, cb_ref, wa_ref, ba_ref, wx_ref,
                         bx_ref, lam_ref, o_ref, hnew_ref, *, first_pos):
    ts = x_ref.shape[0]
    taps = cw_ref.shape[0]
    rows = [buf_ref[n] for n in range(taps - 1)] + [x_ref[n] for n in range(ts)]
    h = h0_ref[...]
    for t in range(ts):
        xc = cb_ref[...] + rows[t] * cw_ref[0:1, :]
        for tap in range(1, taps):
            xc = xc + rows[t + tap] * cw_ref[tap:tap + 1, :]
        reset = jnp.full(xc.shape, True) if first_pos + t == 0 else None
        a, b = _rg_coeffs(xc, wa_ref, ba_ref[...], wx_ref, bx_ref[...], lam_ref[...], reset)
        h = a * h + b
        o_ref[t] = (_gelu_tanh(gr_ref[t]) * h).astype(BF16)
    hnew_ref[...] = h


def _rglru_sample(x_tm, buf_tm, gr_tm, h0, conv_w, conv_b, wa, ba, wx, bx, lam, *, first_pos):
    ts, b, w = x_tm.shape
    return pl.pallas_call(
        functools.partial(_rglru_sample_kernel, first_pos=first_pos),
        out_shape=[jax.ShapeDtypeStruct((ts, b, w), BF16), jax.ShapeDtypeStruct((b, w), F32)],
        compiler_params=pltpu.CompilerParams(vmem_limit_bytes=VMEM_LIMIT),
        name="rglru_sample",
    )(x_tm, buf_tm, gr_tm, h0, conv_w, conv_b, wa, ba, wx, bx, lam)


def _mem_attn_kernel(q_ref, mk_ref, mv_ref, o_ref, *, heads):
    w = q_ref.shape[-1]
    dh = w // heads
    outs = []
    for hh in range(heads):
        sl = slice(hh * dh, (hh + 1) * dh)
        qh = q_ref[:, sl].astype(BF16)
        kh = mk_ref[:, sl].astype(BF16)
        vh = mv_ref[:, sl].astype(BF16)
        s = lax.dot_general(qh, kh, _NT, preferred_element_type=F32) * (dh ** -0.5)
        e = jnp.exp(s - jnp.max(s, axis=-1, keepdims=True))
        p = e / jnp.sum(e, axis=-1, keepdims=True)
        outs.append(jnp.dot(p.astype(BF16), vh, preferred_element_type=F32))
    o_ref[...] = jnp.concatenate(outs, axis=1).astype(BF16)


def _mem_attn_prompt(proj, memkv, *, q_part, heads, tm=512):
    _, t, w = proj.shape
    mtok = memkv.shape[1]
    tm = _row_block(t, tm)
    return pl.pallas_call(
        functools.partial(_mem_attn_kernel, heads=heads),
        grid=(t // tm,),
        in_specs=[
            pl.BlockSpec((None, tm, w), lambda i: (q_part, i, 0)),
            pl.BlockSpec((None, mtok, w), lambda i: (0, 0, 0)),
            pl.BlockSpec((None, mtok, w), lambda i: (1, 0, 0)),
        ],
        out_specs=pl.BlockSpec((tm, w), lambda i: (i, 0)),
        out_shape=jax.ShapeDtypeStruct((t, w), BF16),
        compiler_params=_params("parallel"),
        name="mem_attn_prompt",
    )(proj, memkv, memkv)


def _mem_attn_sample(proj, mem_k, mem_v, *, q_part, heads):
    _, b, ts, w = proj.shape
    mtok = mem_k.shape[1]
    return pl.pallas_call(
        functools.partial(_mem_attn_kernel, heads=heads),
        grid=(b,),
        in_specs=[
            pl.BlockSpec((None, None, ts, w), lambda i: (q_part, i, 0, 0)),
            pl.BlockSpec((None, mtok, w), lambda i: (i, 0, 0)),
            pl.BlockSpec((None, mtok, w), lambda i: (i, 0, 0)),
        ],
        out_specs=pl.BlockSpec((None, ts, w), lambda i: (i, 0, 0)),
        out_shape=jax.ShapeDtypeStruct((b, ts, w), BF16),
        compiler_params=_params("parallel"),
        name="mem_attn_sample",
    )(proj, mem_k, mem_v)


def _merge_kernel(*refs, n_branch, gate_parts):
    o_refs = refs[:n_branch]
    g_refs = refs[n_branch:n_branch + n_branch * gate_parts]
    bg_ref, wb_ref, m_ref = refs[n_branch + n_branch * gate_parts:]
    w = g_refs[0].shape[1]
    for part in range(gate_parts):
        cols = slice(part * w, (part + 1) * w)
        acc = None
        for n in range(n_branch):
            gate = jax.nn.sigmoid(g_refs[n * gate_parts + part][...] + bg_ref[n][:, cols])
            term = gate * jnp.dot(o_refs[n][...], wb_ref[n, :, cols], preferred_element_type=F32)
            acc = term if acc is None else acc + term
        m_ref[:, cols] = acc.astype(BF16)


def _merge(branches, proj, b_gate3, w_branch, *, gate_part0, tm=256):
    m, w = branches[0].shape
    n_branch, _, d = w_branch.shape
    tm = _row_block(m, tm)
    gate_parts = d // w

    def gate_spec(p):
        return pl.BlockSpec((None, tm, w), lambda i: (gate_part0 + p, i, 0))

    resident = dict(pipeline_mode=pl.Buffered(1))
    return pl.pallas_call(
        functools.partial(_merge_kernel, n_branch=n_branch, gate_parts=gate_parts),
        grid=(m // tm,),
        in_specs=[pl.BlockSpec((tm, w), lambda i: (i, 0))] * n_branch
        + [gate_spec(p) for p in range(n_branch * gate_parts)]
        + [pl.BlockSpec((n_branch, 1, d), lambda i: (0, 0, 0), **resident),
           pl.BlockSpec((n_branch, w, d), lambda i: (0, 0, 0), **resident)],
        out_specs=pl.BlockSpec((tm, d), lambda i: (i, 0)),
        out_shape=jax.ShapeDtypeStruct((m, d), BF16),
        compiler_params=_params("parallel"),
        name="merge",
    )(*branches, *([proj] * (n_branch * gate_parts)), b_gate3, w_branch)


def _out_proj_kernel(x_ref, m_ref, w_ref, g_ref, o_ref):
    y = jnp.dot(m_ref[...], w_ref[...], preferred_element_type=F32)
    o_ref[...] = x_ref[...] + _rms(y, g_ref[...])


def _out_proj(x, merged, w_out, g, *, tm=512):
    m, d = x.shape
    tm = _row_block(m, tm)
    return pl.pallas_call(
        _out_proj_kernel,
        grid=(m // tm,),
        in_specs=[
            pl.BlockSpec((tm, d), lambda i: (i, 0)),
            pl.BlockSpec((tm, d), lambda i: (i, 0)),
            pl.BlockSpec((d, d), lambda i: (0, 0)),
            pl.BlockSpec((1, d), lambda i: (0, 0)),
        ],
        out_specs=pl.BlockSpec((tm, d), lambda i: (i, 0)),
        out_shape=jax.ShapeDtypeStruct((m, d), F32),
        compiler_params=_params("parallel"),
        name="out_proj",
    )(x, merged, w_out, g)


def _strict_upper_pair(n):
    j = lax.broadcasted_iota(jnp.int32, (n, n), 0)
    s = lax.broadcasted_iota(jnp.int32, (n, n), 1)
    u = (j > s).astype(BF16)
    return jnp.concatenate([u, u], axis=0)


def kernel(x_prompt, x_sample, cache_sb_k, cache_sb_v, state_conv, state_rglru, cache_mem_k, cache_mem_v, page_table, mem_prompt, ffn1_g_pre, ffn1_g_post, ffn1_w_gu, ffn1_w_down, mix_g_pre, mix_g_post, w_in, b_gate, sb_bias, sb_norm_g, conv_w, conv_b, rg_w_a, rg_b_a, rg_w_x, rg_b_x, rg_lambda, mem_g, w_mem_kv, w_branch, w_out, ffn2_g_pre, ffn2_g_post, ffn2_w_gu, ffn2_w_down):
    bp, tp, d = x_prompt.shape
    bs, ts, _ = x_sample.shape
    assert bp == 1
    n_pool, page, sb_heads, sb_dh = cache_sb_k.shape
    mix_w = sb_heads * sb_dh
    n_pages = page_table.shape[1]
    past_len = n_pages * page
    mem_tok, mem_heads = cache_mem_k.shape[1], cache_mem_k.shape[2]

    row = lambda v: v.reshape(1, -1).astype(F32)
    bf = lambda v: v.astype(BF16)
    w1_gu, w1_down = bf(ffn1_w_gu), bf(ffn1_w_down)
    w2_gu, w2_down = bf(ffn2_w_gu), bf(ffn2_w_down)
    w_in_b, w_mem_b = bf(w_in), bf(w_mem_kv)
    w_branch_b, w_out_b = bf(w_branch), bf(w_out)
    wa_b, wx_b = bf(rg_w_a), bf(rg_w_x)
    b_gate3 = b_gate.reshape(N_BRANCH, 1, d)
    sb_g = row(sb_norm_g)
    rg_vecs = (conv_w, row(conv_b), wa_b, row(rg_b_a), wx_b, row(rg_b_x), row(rg_lambda))
    P_Q, P_K, P_V, P_X, P_GR, P_QM, P_GL = range(7)

    def ffn(x, n):
        if n == 1:
            return _ffn(x, row(ffn1_g_pre), row(ffn1_g_post), w1_gu, w1_down)
        return _ffn(x, row(ffn2_g_pre), row(ffn2_g_post), w2_gu, w2_down)

    def finish(x, branches, proj):
        merged = _merge(branches, proj, b_gate3, w_branch_b, gate_part0=P_GL)
        x = _out_proj(x, merged, w_out_b, row(mix_g_post))
        return ffn(x, 2)

    memkv = _norm_proj(mem_prompt.reshape(mem_tok, d), row(mem_g), w_mem_b, tn=mix_w)
    xp = ffn(x_prompt.reshape(tp, d), 1)
    proj_p, qkv_p = _norm_proj(xp, row(mix_g_pre), w_in_b, tn=mix_w, n_bf16=3,
                               bf16_scale0=sb_dh ** -0.5)
    u_strict2 = _strict_upper_pair(V7X_MXU_DIM)
    u_incl = u_strict2[:V7X_MXU_DIM] + jnp.eye(V7X_MXU_DIM, dtype=BF16)
    o_sb_p = _sb_prompt(qkv_p, sb_bias.astype(F32), u_strict2, u_incl, sb_g, heads=sb_heads)
    o_rg_p, h_p = _rglru_prompt(proj_p, *rg_vecs, x_part=P_X, gate_part=P_GR)
    o_mem_p = _mem_attn_prompt(proj_p, memkv, q_part=P_QM, heads=mem_heads)
    y_p = finish(xp, (o_sb_p, o_rg_p, o_mem_p), proj_p)

    ms = bs * ts
    xs = ffn(x_sample.reshape(ms, d), 1)
    proj_s = _norm_proj(xs, row(mix_g_pre), w_in_b, tn=mix_w)
    proj_s4 = proj_s.reshape(proj_s.shape[0], bs, ts, mix_w)
    ut = _strict_upper_pair(page)
    ut2 = jnp.concatenate([ut[:page].T, ut[:page].T], axis=1)
    bias_cols = jnp.repeat(sb_bias.astype(F32), _QPAD).reshape(1, sb_heads * _QPAD)
    o_sb_s = _sb_sample(proj_s4, cache_sb_k.reshape(n_pool, page * sb_heads, sb_dh),
                        cache_sb_v.reshape(n_pool, page * sb_heads, sb_dh), page_table, bias_cols,
                        ut2, sb_g, heads=sb_heads)
    to_tm = lambda a: a.reshape(bs, ts, mix_w).transpose(1, 0, 2)
    o_rg_tm, h_s = _rglru_sample(to_tm(proj_s[P_X]), state_conv.transpose(1, 0, 2),
                                 to_tm(proj_s[P_GR]), state_rglru, *rg_vecs, first_pos=past_len)
    o_rg_s = o_rg_tm.transpose(1, 0, 2)
    o_mem_s = _mem_attn_sample(proj_s4, cache_mem_k.reshape(bs, mem_tok, mix_w),
                               cache_mem_v.reshape(bs, mem_tok, mix_w), q_part=P_QM,
                               heads=mem_heads)
    branches_s = tuple(o.reshape(ms, mix_w) for o in (o_sb_s, o_rg_s, o_mem_s))
    y_s = finish(xs, branches_s, proj_s)

    n_keep = conv_w.shape[0] - 1
    xr_s = proj_s[P_X].reshape(bs, ts, mix_w)
    conv_s = jnp.concatenate([state_conv.astype(F32), xr_s], axis=1)[:, ts:]
    conv_p = lax.slice(proj_p, (P_X, tp - n_keep, 0), (P_X + 1, tp, mix_w))
    heads4 = lambda a, b_, t_: a.reshape(b_, t_, sb_heads, sb_dh)
    return (
        y_p.reshape(bp, tp, d),
        y_s.reshape(bs, ts, d),
        heads4(proj_p[P_K], bp, tp), heads4(proj_p[P_V], bp, tp),
        heads4(proj_s[P_K], bs, ts), heads4(proj_s[P_V], bs, ts),
        conv_p, conv_s,
        h_p, h_s,
        memkv[0].reshape(bp, mem_tok, mem_heads, mix_w // mem_heads),
        memkv[1].reshape(bp, mem_tok, mem_heads, mix_w // mem_heads),
    )
```

```python
import functools
import math

import jax
import jax.numpy as jnp
from jax import lax
from jax.experimental import pallas as pl
from jax.experimental.pallas import tpu as pltpu

F32 = jnp.float32
BF16 = jnp.bfloat16

EPS = 1e-6
RG_C = 8.0
N_BRANCH = 3
SB_HEADS = 8
MEM_HEADS = 4
RG_BLOCKS = 8

V7X_LANES = 128
V7X_SUBLANES = 8
V7X_MXU_DIM = 256
V7X_VMEM_BYTES = 64 * 1024 * 1024
VMEM_LIMIT = V7X_VMEM_BYTES - 8 * 1024 * 1024

_LOG2E = 1.4426950408889634
_NT = (((1,), (1,)), ((), ()))
_TN = (((0,), (0,)), ((), ()))


def _params(*sem):
    return pltpu.CompilerParams(dimension_semantics=sem, vmem_limit_bytes=VMEM_LIMIT)


def _rms(x, g):
    ms = jnp.mean(x * x, axis=-1, keepdims=True)
    return x * lax.rsqrt(ms + EPS) * g


def _softplus(z):
    return jnp.maximum(z, 0.0) + jnp.log(1.0 + jnp.exp2(jnp.abs(z) * (-_LOG2E)))


def _split_bf16(x):
    hi = x.astype(BF16)
    lo = (x - hi.astype(F32)).astype(BF16)
    return hi, lo


def _row_block(m, target):
    return target if m % target == 0 else m


def _ffn_kernel(x_ref, gpre_ref, gpost_ref, wg_ref, wu_ref, wd_ref, o_ref, h_ref):
    j = pl.program_id(1)

    @pl.when(j == 0)
    def _():
        h_ref[...] = _rms(x_ref[...], gpre_ref[...]).astype(BF16)
        o_ref[...] = jnp.zeros_like(o_ref)

    h = h_ref[...]
    gate = jnp.dot(h, wg_ref[...], preferred_element_type=F32)
    up = jnp.dot(h, wu_ref[...], preferred_element_type=F32)
    act = (gate * jax.nn.sigmoid(gate) * up).astype(BF16)
    o_ref[...] += jnp.dot(act, wd_ref[...], preferred_element_type=F32)

    @pl.when(j == pl.num_programs(1) - 1)
    def _():
        o_ref[...] = x_ref[...] + 0.5 * _rms(o_ref[...], gpost_ref[...])


_FFN_TM = 512
_FFN_TF = 512
_PAGES_PER_STEP = 8


def _ffn(x, g_pre, g_post, w_gu, w_down, *, tm=_FFN_TM, tf=_FFN_TF):
    m, d = x.shape
    f = w_down.shape[0]
    tm = _row_block(m, tm)
    nf = f // tf
    return pl.pallas_call(
        _ffn_kernel,
        grid=(m // tm, nf),
        in_specs=[
            pl.BlockSpec((tm, d), lambda i, j: (i, 0)),
            pl.BlockSpec((1, d), lambda i, j: (0, 0)),
            pl.BlockSpec((1, d), lambda i, j: (0, 0)),
            pl.BlockSpec((d, tf), lambda i, j: (0, j)),
            pl.BlockSpec((d, tf), lambda i, j: (0, j + nf)),
            pl.BlockSpec((tf, d), lambda i, j: (j, 0)),
        ],
        out_specs=pl.BlockSpec((tm, d), lambda i, j: (i, 0)),
        out_shape=jax.ShapeDtypeStruct((m, d), F32),
        scratch_shapes=[pltpu.VMEM((tm, d), BF16)],
        compiler_params=_params("parallel", "arbitrary"),
        name="ffn",
    )(x, g_pre, g_post, w_gu, w_gu, w_down)


def _norm_proj_kernel(x_ref, g_ref, w_ref, o_ref, *rest, n_bf16, bf16_scale0):
    if n_bf16:
        obf_ref, h_ref = rest
    else:
        (h_ref,) = rest
    j = pl.program_id(1)

    @pl.when(j == 0)
    def _():
        h_ref[...] = _rms(x_ref[...], g_ref[...]).astype(BF16)

    y = jnp.dot(h_ref[...], w_ref[...], preferred_element_type=F32)
    o_ref[...] = y
    if n_bf16:
        @pl.when(j < n_bf16)
        def _():
            obf_ref[...] = (y * jnp.where(j == 0, bf16_scale0, 1.0)).astype(BF16)


def _norm_proj(x, g, w, *, tn, n_bf16=0, bf16_scale0=1.0, tm=1024):
    m, d = x.shape
    n = w.shape[1]
    tm = _row_block(m, tm)
    parts = n // tn
    out_shape = [jax.ShapeDtypeStruct((parts, m, tn), F32)]
    out_specs = [pl.BlockSpec((None, tm, tn), lambda i, j: (j, i, 0))]
    if n_bf16:
        out_shape.append(jax.ShapeDtypeStruct((n_bf16, m, tn), BF16))
        out_specs.append(
            pl.BlockSpec((None, tm, tn), lambda i, j: (jnp.minimum(j, n_bf16 - 1), i, 0)))
    res = pl.pallas_call(
        functools.partial(_norm_proj_kernel, n_bf16=n_bf16, bf16_scale0=bf16_scale0),
        grid=(m // tm, parts),
        in_specs=[
            pl.BlockSpec((tm, d), lambda i, j: (i, 0)),
            pl.BlockSpec((1, d), lambda i, j: (0, 0)),
            pl.BlockSpec((d, tn), lambda i, j: (0, j)),
        ],
        out_specs=out_specs,
        out_shape=out_shape,
        scratch_shapes=[pltpu.VMEM((tm, d), BF16)],
        compiler_params=_params("parallel", "arbitrary"),
        name="norm_proj",
    )(x, g, w)
    return res if n_bf16 else res[0]


_BIAS_TERMS = 3


def _sb_prompt_kernel(bias_ref, q_ref, k_ref, v_ref, u2_ref, ui_ref, g_ref, o_ref, acc_ref, c_ref,
                      kaug_ref, *, bq, bk, fill_rows):
    h = pl.program_id(0)
    i = pl.program_id(1)
    bias = bias_ref[h]
    r = bq // bk
    t, dh = k_ref.shape

    @pl.when(i == 0)
    def _():
        lane = lax.broadcasted_iota(jnp.int32, (fill_rows, dh), 1)
        rem = jnp.full((fill_rows, dh), bias, F32)
        cols = jnp.zeros((fill_rows, dh), F32)
        for n in range(_BIAS_TERMS):
            term = rem.astype(BF16).astype(F32)
            cols = jnp.where(lane == n, term, cols)
            rem = rem - term
        cols = cols.astype(BF16)
        for start in range(0, t, fill_rows):
            kaug_ref[start:start + fill_rows, 0:dh] = k_ref[start:start + fill_rows, :]
            kaug_ref[start:start + fill_rows, dh:2 * dh] = cols

    ones = jnp.where(lax.broadcasted_iota(jnp.int32, (bq, dh), 1) < _BIAS_TERMS, 1.0, 0.0)
    q = jnp.concatenate([q_ref[...], ones.astype(BF16)], axis=1)
    sp0 = _softplus(jnp.full((1, bk), bias, F32))
    col_const = sp0 * (bk - lax.broadcasted_iota(jnp.int32, (1, bk), 1)).astype(F32)

    def scores(kb, q_rows):
        start = pl.multiple_of(kb * bk, bk)
        return lax.dot_general(q_rows, kaug_ref[pl.ds(start, bk), :], _NT,
                               preferred_element_type=F32), v_ref[pl.ds(start, bk), :]

    def diag_tile(kb, q_rows, c):
        z, vblk = scores(kb, q_rows)
        keep = (lax.broadcasted_iota(jnp.int32, (bk, bk), 1)
                < lax.broadcasted_iota(jnp.int32, (bk, bk), 0))
        sp = jnp.where(keep, _softplus(z), 0.0)
        hi, lo = _split_bf16(sp)
        excl = jnp.dot(jnp.concatenate([hi, lo], axis=1), u2_ref[...], preferred_element_type=F32)
        w = jnp.where(keep, jnp.exp(z - sp - excl - c), 0.0)
        pv = jnp.dot(w.astype(BF16), vblk, preferred_element_type=F32)
        return pv, jnp.sum(sp, axis=1, keepdims=True)

    def plain_block(kb, q_rows, c):
        z, vblk = scores(kb, q_rows)
        d = (_softplus(z) - sp0).astype(BF16)
        incl = jnp.dot(d, ui_ref[...], preferred_element_type=F32)
        w = jnp.exp(z - incl - col_const - c)
        pv = jnp.dot(w.astype(BF16), vblk, preferred_element_type=F32)
        return pv, incl[:, 0:1] + col_const[:, 0:1]

    acc_ref[...] = jnp.zeros_like(acc_ref)
    c_ref[...] = jnp.zeros_like(c_ref)
    for sub in reversed(range(r)):
        r0, r1 = sub * bk, (sub + 1) * bk
        pv, tot = diag_tile(i * r + sub, q[r0:r1, :], c_ref[r0:r1, :])
        acc_ref[r0:r1, :] += pv
        c_ref[r0:r1, :] += tot
        if r1 < bq:
            pv, tot = plain_block(i * r + sub, q[r1:, :], c_ref[r1:, :])
            acc_ref[r1:, :] += pv
            c_ref[r1:, :] += tot

    def body(n, carry):
        c = c_ref[...]
        pvs = None
        for d in range(r):
            pv, tot = plain_block((i - n) * r - 1 - d, q, c)
            pvs = pv if pvs is None else pvs + pv
            c = c + tot
        acc_ref[...] += pvs
        c_ref[...] = c
        return carry

    lax.fori_loop(0, i, body, 0)
    o_ref[...] = _rms(acc_ref[...], g_ref[...]).astype(BF16)


def _sb_prompt(qkv, bias, u_strict2, u_incl, g, *, heads, bq=2048, bk=V7X_MXU_DIM):
    _, t, w = qkv.shape
    dh = w // heads
    bq = min(bq, t)
    fill_rows = min(t, 2048)
    assert t % bq == 0 and bq % bk == 0 and t % fill_rows == 0
    return pl.pallas_call(
        functools.partial(_sb_prompt_kernel, bq=bq, bk=bk, fill_rows=fill_rows),
        grid=(heads, t // bq),
        in_specs=[
            pl.BlockSpec(memory_space=pltpu.SMEM),
            pl.BlockSpec((None, bq, dh), lambda h, i: (0, i, h)),
            pl.BlockSpec((None, t, dh), lambda h, i: (1, 0, h)),
            pl.BlockSpec((None, t, dh), lambda h, i: (2, 0, h)),
            pl.BlockSpec((2 * bk, bk), lambda h, i: (0, 0)),
            pl.BlockSpec((bk, bk), lambda h, i: (0, 0)),
            pl.BlockSpec((1, dh), lambda h, i: (0, 0)),
        ],
        out_specs=pl.BlockSpec((bq, dh), lambda h, i: (i, h)),
        out_shape=jax.ShapeDtypeStruct((t, w), BF16),
        scratch_shapes=[pltpu.VMEM((bq, dh), F32), pltpu.VMEM((bq, 1), F32),
                        pltpu.VMEM((t, 2 * dh), BF16)],
        compiler_params=_params("arbitrary", "arbitrary"),
        name="sb_prompt",
    )(bias, qkv, qkv, qkv, u_strict2, u_incl, g)


_QPAD = 8


def _sb_sample_phases(q_ref, kn_ref, vn_ref, bias_ref, ut_ref, g_ref, kv_refs, o_ref, qrows_ref,
                      acc_ref, c_ref, kpad_ref, vpad_ref, *, heads, scale):
    pages_per_step = len(kv_refs) // 2
    ts, w = q_ref.shape
    dh = w // heads
    ncol = heads * _QPAD
    page = kpad_ref.shape[0]

    def by_key(ref):
        parts = [ref[pl.ds(hh, page, stride=heads), :] for hh in range(heads)]
        return jnp.concatenate(parts, axis=1).astype(BF16)

    def log_weights(kbs, keep=None):
        zs = [lax.dot_general(kb, qrows_ref[...], _NT, preferred_element_type=F32) * scale
              + bias_ref[...] for kb in kbs]
        sps = [_softplus(z) for z in zs]
        if keep is not None:
            sps = [jnp.where(keep, sp, 0.0) for sp in sps]
        excls = [jnp.dot(ut_ref[...], jnp.concatenate(_split_bf16(sp), axis=0),
                         preferred_element_type=F32) for sp in sps]
        return ([z - sp - excl for z, sp, excl in zip(zs, sps, excls)],
                [jnp.sum(sp, axis=0, keepdims=True) for sp in sps])

    def start():
        q8 = jnp.concatenate([q_ref[...], jnp.zeros((_QPAD - ts, w), F32)], axis=0)
        qt = jnp.concatenate([q8] * heads, axis=0)
        row_head = lax.broadcasted_iota(jnp.int32, (ncol, w), 0) // _QPAD
        col_head = lax.broadcasted_iota(jnp.int32, (ncol, w), 1) // dh
        qrows_ref[...] = jnp.where(row_head == col_head, qt, 0.0).astype(BF16)
        kpad_ref[...] = jnp.zeros_like(kpad_ref)
        vpad_ref[...] = jnp.zeros_like(vpad_ref)
        kpad_ref[0:ts, :] = kn_ref[...]
        vpad_ref[0:ts, :] = vn_ref[...]
        key = lax.broadcasted_iota(jnp.int32, (page, ncol), 0)
        qry = lax.broadcasted_iota(jnp.int32, (page, ncol), 1) % _QPAD
        keep = (key < qry) & (qry < ts)
        (arg,), (tot,) = log_weights([kpad_ref[...].astype(BF16)], keep)
        wgt = jnp.where(keep, jnp.exp(arg), 0.0).astype(BF16)
        acc_ref[...] = lax.dot_general(wgt, vpad_ref[...].astype(BF16), _TN,
                                       preferred_element_type=F32)
        c_ref[...] = tot

    def page_scores():
        return [lax.dot_general(by_key(kv_refs[r]), qrows_ref[...], _NT,
                                preferred_element_type=F32) * scale + bias_ref[...]
                for r in range(pages_per_step)]

    def page_cumsums(zs):
        sps = [_softplus(z) for z in zs]
        excls = [jnp.dot(ut_ref[...], jnp.concatenate(_split_bf16(sp), axis=0),
                         preferred_element_type=F32) for sp in sps]
        return ([z - sp - excl for z, sp, excl in zip(zs, sps, excls)],
                [jnp.sum(sp, axis=0, keepdims=True) for sp in sps])

    def page_sum(args_tots, valid):
        c = c_ref[...]
        wgts = []
        for arg, tot in zip(*args_tots):
            wgts.append((jnp.exp(arg - c) * valid).astype(BF16))
            c = c + tot * valid
        c_ref[...] = c
        vals = [by_key(kv_refs[pages_per_step + r]) for r in range(pages_per_step)]
        acc_ref[...] += lax.dot_general(jnp.concatenate(wgts, axis=0),
                                        jnp.concatenate(vals, axis=0), _TN,
                                        preferred_element_type=F32)

    def finish():
        outs = []
        for hh in range(heads):
            blk = acc_ref[hh * _QPAD:hh * _QPAD + ts, hh * dh:(hh + 1) * dh]
            outs.append(_rms(blk, g_ref[...]))
        o_ref[...] = jnp.concatenate(outs, axis=1).astype(BF16)

    return start, (page_scores, page_cumsums, page_sum), finish


def _sb_sample_kernel(pt_ref, q_ref, kn_ref, vn_ref, bias_ref, ut_ref, g_ref, *rest,
                      pages_per_step, heads, scale):
    kv_refs = rest[:2 * pages_per_step]
    o_ref, qrows_ref, acc_ref, c_ref, kpad_ref, vpad_ref = rest[2 * pages_per_step:]
    del pt_ref
    s = pl.program_id(1)
    start, (page_scores, page_cumsums, page_sum), finish = _sb_sample_phases(
        q_ref, kn_ref, vn_ref, bias_ref, ut_ref, g_ref, kv_refs, o_ref, qrows_ref, acc_ref,
        c_ref, kpad_ref, vpad_ref, heads=heads, scale=scale)
    pl.when(s == 0)(start)
    page_sum(page_cumsums(page_scores()), 1.0)
    pl.when(s == pl.num_programs(1) - 1)(finish)


def _sb_sample(proj, cache_k, cache_v, page_table, bias_cols, ut2, g, *, heads, b0, n_batches,
               pages_per_step=16):
    _, _, ts, w = proj.shape
    dh = w // heads
    page = cache_k.shape[1] // heads
    n_pages = page_table.shape[1]
    assert ts <= _QPAD and dh == V7X_LANES and cache_k.shape[2] == dh
    gp = pages_per_step if n_pages % pages_per_step == 0 else 1
    ncol = heads * _QPAD

    def page_spec(r):
        return pl.BlockSpec(
            (None, page * heads, dh),
            lambda bi, s, pt: (pt[b0 + bi, n_pages - 1 - (s * gp + r)], 0, 0))

    def tok_spec(p):
        return pl.BlockSpec((None, None, ts, w), lambda bi, s, pt: (p, b0 + bi, 0, 0))

    grid_spec = pltpu.PrefetchScalarGridSpec(
        num_scalar_prefetch=1,
        grid=(n_batches, n_pages // gp),
        in_specs=[
            tok_spec(0), tok_spec(1), tok_spec(2),
            pl.BlockSpec((1, ncol), lambda bi, s, pt: (0, 0)),
            pl.BlockSpec((page, 2 * page), lambda bi, s, pt: (0, 0)),
            pl.BlockSpec((1, dh), lambda bi, s, pt: (0, 0)),
        ] + [page_spec(r) for r in range(gp)] * 2,
        out_specs=pl.BlockSpec((None, ts, w), lambda bi, s, pt: (bi, 0, 0)),
        scratch_shapes=[
            pltpu.VMEM((ncol, w), BF16),
            pltpu.VMEM((ncol, w), F32),
            pltpu.VMEM((1, ncol), F32),
            pltpu.VMEM((page, w), F32),
            pltpu.VMEM((page, w), F32),
        ],
    )
    return pl.pallas_call(
        functools.partial(_sb_sample_kernel, pages_per_step=gp, heads=heads, scale=dh ** -0.5),
        grid_spec=grid_spec,
        out_shape=jax.ShapeDtypeStruct((n_batches, ts, w), BF16),
        compiler_params=_params("arbitrary", "arbitrary"),
        name="sb_sample",
    )(page_table, proj, proj, proj, bias_cols, ut2, g,
      *([cache_k] * gp), *([cache_v] * gp))


def _ffn_sb_kernel(pt_ref, x_ref, gpre_ref, gpost_ref, wg_ref, wu_ref, wd_ref,
                   q_ref, kn_ref, vn_ref, bias_ref, ut_ref, g_ref, *rest,
                   pages_per_step, heads, scale, steps_per_batch, n_batches):
    kv_refs = rest[:2 * pages_per_step]
    o_ref, osb_ref, h_ref, qrows_ref, acc_ref, c_ref, kpad_ref, vpad_ref = rest[2 * pages_per_step:]
    del pt_ref
    j = pl.program_id(1)
    last_j = pl.num_programs(1) - 1
    step = pl.program_id(0) * pl.num_programs(1) + j
    n_valid = n_batches * steps_per_batch
    valid = step < n_valid
    s = jnp.minimum(step, n_valid - 1) % steps_per_batch
    start, (page_scores, page_cumsums, page_sum), finish = _sb_sample_phases(
        q_ref, kn_ref, vn_ref, bias_ref, ut_ref, g_ref, kv_refs, osb_ref, qrows_ref, acc_ref,
        c_ref, kpad_ref, vpad_ref, heads=heads, scale=scale)

    @pl.when(j == 0)
    def _():
        h_ref[...] = _rms(x_ref[...], gpre_ref[...]).astype(BF16)
        o_ref[...] = jnp.zeros_like(o_ref)

    pl.when(valid & (s == 0))(start)

    zs = page_scores()
    h = h_ref[...]
    gate = jnp.dot(h, wg_ref[...], preferred_element_type=F32)
    up = jnp.dot(h, wu_ref[...], preferred_element_type=F32)
    args_tots = page_cumsums(zs)
    act = (gate * jax.nn.sigmoid(gate) * up).astype(BF16)
    o_ref[...] += jnp.dot(act, wd_ref[...], preferred_element_type=F32)
    page_sum(args_tots, jnp.where(valid, 1.0, 0.0))

    @pl.when(j == last_j)
    def _():
        o_ref[...] = x_ref[...] + 0.5 * _rms(o_ref[...], gpost_ref[...])

    pl.when(valid & (s == steps_per_batch - 1))(finish)


def _ffn_sb(x, g_pre, g_post, w_gu, w_down, proj, cache_k, cache_v, page_table, bias_cols, ut2, g,
            *, heads, b0, n_batches, pages_per_step, tm=_FFN_TM, tf=_FFN_TF):
    m, d = x.shape
    f = w_down.shape[0]
    tm = _row_block(m, tm)
    nf = f // tf
    _, _, ts, w = proj.shape
    dh = w // heads
    page = cache_k.shape[1] // heads
    n_pages = page_table.shape[1]
    gp = pages_per_step
    spb = n_pages // gp
    ncol = heads * _QPAD
    assert ts <= _QPAD and dh == V7X_LANES and cache_k.shape[2] == dh and n_pages % gp == 0
    assert 0 < n_batches * spb <= (m // tm) * nf

    n_steps = (m // tm) * nf
    st = jnp.minimum(jnp.arange(n_steps, dtype=jnp.int32), n_batches * spb - 1)
    rel_b, s_in_b = st // spb, st % spb
    slots = n_pages - 1 - (s_in_b[:, None] * gp + jnp.arange(gp, dtype=jnp.int32)[None, :])
    page_ids = page_table[b0 + rel_b[:, None], slots].astype(jnp.int32)
    sched = jnp.concatenate([rel_b[:, None], page_ids], axis=1).reshape(-1)
    width = gp + 1

    def page_spec(r):
        return pl.BlockSpec((None, page * heads, dh),
                            lambda i, j, sc: (sc[(i * nf + j) * width + 1 + r], 0, 0))

    def tok_spec(p):
        return pl.BlockSpec((None, None, ts, w),
                            lambda i, j, sc: (p, b0 + sc[(i * nf + j) * width], 0, 0))

    grid_spec = pltpu.PrefetchScalarGridSpec(
        num_scalar_prefetch=1,
        grid=(m // tm, nf),
        in_specs=[
            pl.BlockSpec((tm, d), lambda i, j, pt: (i, 0)),
            pl.BlockSpec((1, d), lambda i, j, pt: (0, 0)),
            pl.BlockSpec((1, d), lambda i, j, pt: (0, 0)),
            pl.BlockSpec((d, tf), lambda i, j, pt: (0, j)),
            pl.BlockSpec((d, tf), lambda i, j, pt: (0, j + nf)),
            pl.BlockSpec((tf, d), lambda i, j, pt: (j, 0)),
            tok_spec(0), tok_spec(1), tok_spec(2),
            pl.BlockSpec((1, ncol), lambda i, j, pt: (0, 0)),
            pl.BlockSpec((page, 2 * page), lambda i, j, pt: (0, 0)),
            pl.BlockSpec((1, dh), lambda i, j, pt: (0, 0)),
        ] + [page_spec(r) for r in range(gp)] * 2,
        out_specs=[
            pl.BlockSpec((tm, d), lambda i, j, sc: (i, 0)),
            pl.BlockSpec((None, ts, w), lambda i, j, sc: (sc[(i * nf + j) * width], 0, 0)),
        ],
        scratch_shapes=[
            pltpu.VMEM((tm, d), BF16),
            pltpu.VMEM((ncol, w), BF16),
            pltpu.VMEM((ncol, w), F32),
            pltpu.VMEM((1, ncol), F32),
            pltpu.VMEM((page, w), F32),
            pltpu.VMEM((page, w), F32),
        ],
    )
    return pl.pallas_call(
        functools.partial(_ffn_sb_kernel, pages_per_step=gp, heads=heads, scale=dh ** -0.5,
                          steps_per_batch=spb, n_batches=n_batches),
        grid_spec=grid_spec,
        out_shape=[jax.ShapeDtypeStruct((m, d), F32),
                   jax.ShapeDtypeStruct((n_batches, ts, w), BF16)],
        compiler_params=_params("arbitrary", "arbitrary"),
        name="ffn_sb",
    )(sched, x, g_pre, g_post, w_gu, w_gu, w_down, proj, proj, proj, bias_cols, ut2, g,
      *([cache_k] * gp), *([cache_v] * gp))


def _gelu_tanh(x):
    return 0.5 * x * (1.0 + jnp.tanh(math.sqrt(2.0 / math.pi) * (x + 0.044715 * (x * x * x))))


def _rg_coeffs(xc, wa_ref, ba, wx_ref, bx, lam, reset=None):
    blocks = wa_ref.shape[0]
    bw = wa_ref.shape[1]
    ra, rx = [], []
    for n in range(blocks):
        xb = xc[:, n * bw:(n + 1) * bw].astype(BF16)
        ra.append(jnp.dot(xb, wa_ref[n], preferred_element_type=F32))
        rx.append(jnp.dot(xb, wx_ref[n], preferred_element_type=F32))
    r = jax.nn.sigmoid(jnp.concatenate(ra, axis=1) + ba)
    ig = jax.nn.sigmoid(jnp.concatenate(rx, axis=1) + bx)
    log_a = -RG_C * r * _softplus(-lam)
    a = jnp.exp(log_a)
    y = 2.0 * log_a
    u = jnp.exp(y)
    mid = (u < 1.0) & (u > 0.0)
    ratio = (1.0 - u) * y / jnp.log(jnp.where(mid, u, 0.5))
    neg_expm1 = jnp.where(mid, ratio, jnp.where(u > 0.0, -y, 1.0))
    mult = jnp.sqrt(neg_expm1)
    if reset is not None:
        a = jnp.where(reset, 0.0, a)
        mult = jnp.where(reset, 1.0, mult)
    return a, mult * ig * xc


def _rglru_prompt_kernel(x_ref, gr_ref, cw_ref, cb_ref, wa_ref, ba_ref, wx_ref, bx_ref, lam_ref,
                         o_ref, hlast_ref, xbuf_ref, a_ref, b_ref, hs_ref, h_ref, *, taps):
    step = pl.program_id(0)
    tc = x_ref.shape[0]
    pad = V7X_SUBLANES

    @pl.when(step == 0)
    def _():
        xbuf_ref[0:pad, :] = jnp.zeros((pad, x_ref.shape[1]), F32)
        h_ref[...] = jnp.zeros_like(h_ref)

    xbuf_ref[pad:pad + tc, :] = x_ref[...]
    base = pad - (taps - 1)
    xc = cb_ref[...] + xbuf_ref[base:base + tc, :] * cw_ref[0:1, :]
    for tap in range(1, taps):
        xc = xc + xbuf_ref[base + tap:base + tap + tc, :] * cw_ref[tap:tap + 1, :]
    reset = (step * tc + lax.broadcasted_iota(jnp.int32, xc.shape, 0)) == 0
    a, b = _rg_coeffs(xc, wa_ref, ba_ref[...], wx_ref, bx_ref[...], lam_ref[...], reset)
    a_ref[...] = a
    b_ref[...] = b

    def body(t, h):
        h = a_ref[pl.ds(t, 1), :] * h + b_ref[pl.ds(t, 1), :]
        hs_ref[pl.ds(t, 1), :] = h
        return h

    h = lax.fori_loop(0, tc, body, h_ref[...], unroll=8)
    h_ref[...] = h
    hlast_ref[...] = h
    o_ref[...] = (_gelu_tanh(gr_ref[...]) * hs_ref[...]).astype(BF16)
    xbuf_ref[0:pad, :] = xbuf_ref[tc:tc + pad, :]


def _rglru_prompt(proj, conv_w, conv_b, wa, ba, wx, bx, lam, *, x_part, gate_part, tc=512):
    _, t, w = proj.shape
    taps = conv_w.shape[0]
    tc = _row_block(t, tc)
    assert tc % V7X_SUBLANES == 0 and taps - 1 <= V7X_SUBLANES
    const2 = lambda i: (0, 0)
    const3 = lambda i: (0, 0, 0)
    return pl.pallas_call(
        functools.partial(_rglru_prompt_kernel, taps=taps),
        grid=(t // tc,),
        in_specs=[
            pl.BlockSpec((None, tc, w), lambda i: (x_part, i, 0)),
            pl.BlockSpec((None, tc, w), lambda i: (gate_part, i, 0)),
            pl.BlockSpec(conv_w.shape, const2),
            pl.BlockSpec((1, w), const2),
            pl.BlockSpec(wa.shape, const3),
            pl.BlockSpec((1, w), const2),
            pl.BlockSpec(wx.shape, const3),
            pl.BlockSpec((1, w), const2),
            pl.BlockSpec((1, w), const2),
        ],
        out_specs=[pl.BlockSpec((tc, w), lambda i: (i, 0)), pl.BlockSpec((1, w), const2)],
        out_shape=[jax.ShapeDtypeStruct((t, w), BF16), jax.ShapeDtypeStruct((1, w), F32)],
        scratch_shapes=[
            pltpu.VMEM((tc + V7X_SUBLANES, w), F32),
            pltpu.VMEM((tc, w), F32),
            pltpu.VMEM((tc, w), F32),
            pltpu.VMEM((tc, w), F32),
            pltpu.VMEM((1, w), F32),
        ],
        compiler_params=_params("arbitrary"),
        name="rglru_prompt",
    )(proj, proj, conv_w, conv_b, wa, ba, wx, bx, lam)


def _rglru_sample_kernel(x_ref, buf_ref, gr_ref, h0_ref, cw_ref, cb_ref, wa_ref, ba_ref, wx_ref,
                         bx_ref, lam_ref, o_ref, hnew_ref, *, first_pos):
    ts = x_ref.shape[0]
    taps = cw_ref.shape[0]
    rows = [buf_ref[n] for n in range(taps - 1)] + [x_ref[n] for n in range(ts)]
    h = h0_ref[...]
    for t in range(ts):
        xc = cb_ref[...] + rows[t] * cw_ref[0:1, :]
        for tap in range(1, taps):
            xc = xc + rows[t + tap] * cw_ref[tap:tap + 1, :]
        reset = jnp.full(xc.shape, True) if first_pos + t == 0 else None
        a, b = _rg_coeffs(xc, wa_ref, ba_ref[...], wx_ref, bx_ref[...], lam_ref[...], reset)
        h = a * h + b
        o_ref[t] = (_gelu_tanh(gr_ref[t]) * h).astype(BF16)
    hnew_ref[...] = h


def _rglru_sample(x_tm, buf_tm, gr_tm, h0, conv_w, conv_b, wa, ba, wx, bx, lam, *, first_pos):
    ts, b, w = x_tm.shape
    return pl.pallas_call(
        functools.partial(_rglru_sample_kernel, first_pos=first_pos),
        out_shape=[jax.ShapeDtypeStruct((ts, b, w), BF16), jax.ShapeDtypeStruct((b, w), F32)],
        compiler_params=pltpu.CompilerParams(vmem_limit_bytes=VMEM_LIMIT),
        name="rglru_sample",
    )(x_tm, buf_tm, gr_tm, h0, conv_w, conv_b, wa, ba, wx, bx, lam)


def _mem_attn_kernel(q_ref, mk_ref, mv_ref, o_ref, *, heads):
    w = q_ref.shape[-1]
    dh = w // heads
    outs = []
    for hh in range(heads):
        sl = slice(hh * dh, (hh + 1) * dh)
        qh = q_ref[:, sl].astype(BF16)
        kh = mk_ref[:, sl].astype(BF16)
        vh = mv_ref[:, sl].astype(BF16)
        s = lax.dot_general(qh, kh, _NT, preferred_element_type=F32) * (dh ** -0.5)
        e = jnp.exp(s - jnp.max(s, axis=-1, keepdims=True))
        p = e / jnp.sum(e, axis=-1, keepdims=True)
        outs.append(jnp.dot(p.astype(BF16), vh, preferred_element_type=F32))
    o_ref[...] = jnp.concatenate(outs, axis=1).astype(BF16)


def _mem_attn_prompt(proj, memkv, *, q_part, heads, tm=512):
    _, t, w = proj.shape
    mtok = memkv.shape[1]
    tm = _row_block(t, tm)
    return pl.pallas_call(
        functools.partial(_mem_attn_kernel, heads=heads),
        grid=(t // tm,),
        in_specs=[
            pl.BlockSpec((None, tm, w), lambda i: (q_part, i, 0)),
            pl.BlockSpec((None, mtok, w), lambda i: (0, 0, 0)),
            pl.BlockSpec((None, mtok, w), lambda i: (1, 0, 0)),
        ],
        out_specs=pl.BlockSpec((tm, w), lambda i: (i, 0)),
        out_shape=jax.ShapeDtypeStruct((t, w), BF16),
        compiler_params=_params("parallel"),
        name="mem_attn_prompt",
    )(proj, memkv, memkv)


def _mem_attn_sample(proj, mem_k, mem_v, *, q_part, heads):
    _, b, ts, w = proj.shape
    mtok = mem_k.shape[1]
    return pl.pallas_call(
        functools.partial(_mem_attn_kernel, heads=heads),
        grid=(b,),
        in_specs=[
            pl.BlockSpec((None, None, ts, w), lambda i: (q_part, i, 0, 0)),
            pl.BlockSpec((None, mtok, w), lambda i: (i, 0, 0)),
            pl.BlockSpec((None, mtok, w), lambda i: (i, 0, 0)),
        ],
        out_specs=pl.BlockSpec((None, ts, w), lambda i: (i, 0, 0)),
        out_shape=jax.ShapeDtypeStruct((b, ts, w), BF16),
        compiler_params=_params("parallel"),
        name="mem_attn_sample",
    )(proj, mem_k, mem_v)


def _merge_kernel(*refs, n_branch, gate_parts):
    o_refs = refs[:n_branch]
    g_refs = refs[n_branch:n_branch + n_branch * gate_parts]
    bg_ref, wb_ref, m_ref = refs[n_branch + n_branch * gate_parts:]
    w = g_refs[0].shape[1]
    for part in range(gate_parts):
        cols = slice(part * w, (part + 1) * w)
        acc = None
        for n in range(n_branch):
            gate = jax.nn.sigmoid(g_refs[n * gate_parts + part][...] + bg_ref[n][:, cols])
            term = gate * jnp.dot(o_refs[n][...], wb_ref[n, :, cols], preferred_element_type=F32)
            acc = term if acc is None else acc + term
        m_ref[:, cols] = acc.astype(BF16)


def _merge(branches, proj, b_gate3, w_branch, *, gate_part0, tm=256):
    m, w = branches[0].shape
    n_branch, _, d = w_branch.shape
    tm = _row_block(m, tm)
    gate_parts = d // w

    def gate_spec(p):
        return pl.BlockSpec((None, tm, w), lambda i: (gate_part0 + p, i, 0))

    resident = dict(pipeline_mode=pl.Buffered(1))
    return pl.pallas_call(
        functools.partial(_merge_kernel, n_branch=n_branch, gate_parts=gate_parts),
        grid=(m // tm,),
        in_specs=[pl.BlockSpec((tm, w), lambda i: (i, 0))] * n_branch
        + [gate_spec(p) for p in range(n_branch * gate_parts)]
        + [pl.BlockSpec((n_branch, 1, d), lambda i: (0, 0, 0), **resident),
           pl.BlockSpec((n_branch, w, d), lambda i: (0, 0, 0), **resident)],
        out_specs=pl.BlockSpec((tm, d), lambda i: (i, 0)),
        out_shape=jax.ShapeDtypeStruct((m, d), BF16),
        compiler_params=_params("parallel"),
        name="merge",
    )(*branches, *([proj] * (n_branch * gate_parts)), b_gate3, w_branch)


def _out_proj_kernel(x_ref, m_ref, w_ref, g_ref, o_ref):
    y = jnp.dot(m_ref[...], w_ref[...], preferred_element_type=F32)
    o_ref[...] = x_ref[...] + _rms(y, g_ref[...])


def _out_proj(x, merged, w_out, g, *, tm=512):
    m, d = x.shape
    tm = _row_block(m, tm)
    return pl.pallas_call(
        _out_proj_kernel,
        grid=(m // tm,),
        in_specs=[
            pl.BlockSpec((tm, d), lambda i: (i, 0)),
            pl.BlockSpec((tm, d), lambda i: (i, 0)),
            pl.BlockSpec((d, d), lambda i: (0, 0)),
            pl.BlockSpec((1, d), lambda i: (0, 0)),
        ],
        out_specs=pl.BlockSpec((tm, d), lambda i: (i, 0)),
        out_shape=jax.ShapeDtypeStruct((m, d), F32),
        compiler_params=_params("parallel"),
        name="out_proj",
    )(x, merged, w_out, g)


def _strict_upper_pair(n):
    j = lax.broadcasted_iota(jnp.int32, (n, n), 0)
    s = lax.broadcasted_iota(jnp.int32, (n, n), 1)
    u = (j > s).astype(BF16)
    return jnp.concatenate([u, u], axis=0)


def kernel(x_prompt, x_sample, cache_sb_k, cache_sb_v, state_conv, state_rglru, cache_mem_k, cache_mem_v, page_table, mem_prompt, ffn1_g_pre, ffn1_g_post, ffn1_w_gu, ffn1_w_down, mix_g_pre, mix_g_post, w_in, b_gate, sb_bias, sb_norm_g, conv_w, conv_b, rg_w_a, rg_b_a, rg_w_x, rg_b_x, rg_lambda, mem_g, w_mem_kv, w_branch, w_out, ffn2_g_pre, ffn2_g_post, ffn2_w_gu, ffn2_w_down):
    bp, tp, d = x_prompt.shape
    bs, ts, _ = x_sample.shape
    assert bp == 1
    n_pool, page, sb_heads, sb_dh = cache_sb_k.shape
    mix_w = sb_heads * sb_dh
    n_pages = page_table.shape[1]
    past_len = n_pages * page
    mem_tok, mem_heads = cache_mem_k.shape[1], cache_mem_k.shape[2]

    row = lambda v: v.reshape(1, -1).astype(F32)
    bf = lambda v: v.astype(BF16)
    w1_gu, w1_down = bf(ffn1_w_gu), bf(ffn1_w_down)
    w2_gu, w2_down = bf(ffn2_w_gu), bf(ffn2_w_down)
    w_in_b, w_mem_b = bf(w_in), bf(w_mem_kv)
    w_branch_b, w_out_b = bf(w_branch), bf(w_out)
    wa_b, wx_b = bf(rg_w_a), bf(rg_w_x)
    b_gate3 = b_gate.reshape(N_BRANCH, 1, d)
    sb_g = row(sb_norm_g)
    rg_vecs = (conv_w, row(conv_b), wa_b, row(rg_b_a), wx_b, row(rg_b_x), row(rg_lambda))
    P_Q, P_K, P_V, P_X, P_GR, P_QM, P_GL = range(7)

    ffn1 = (row(ffn1_g_pre), row(ffn1_g_post), w1_gu, w1_down)
    ffn2 = (row(ffn2_g_pre), row(ffn2_g_post), w2_gu, w2_down)

    def mix(x, branches, proj):
        merged = _merge(branches, proj, b_gate3, w_branch_b, gate_part0=P_GL)
        return _out_proj(x, merged, w_out_b, row(mix_g_post))

    ms = bs * ts
    xs = _ffn(x_sample.reshape(ms, d), *ffn1)
    proj_s = _norm_proj(xs, row(mix_g_pre), w_in_b, tn=mix_w)
    proj_s4 = proj_s.reshape(proj_s.shape[0], bs, ts, mix_w)
    ut = _strict_upper_pair(page)
    ut2 = jnp.concatenate([ut[:page].T, ut[:page].T], axis=1)
    bias_cols = jnp.repeat(sb_bias.astype(F32), _QPAD).reshape(1, sb_heads * _QPAD)
    gp = _PAGES_PER_STEP if n_pages % _PAGES_PER_STEP == 0 else 1
    steps = (tp // _row_block(tp, _FFN_TM)) * (w1_down.shape[0] // _FFN_TF)
    nb1 = min(bs, steps // (n_pages // gp))
    sb_args = (proj_s4, cache_sb_k.reshape(n_pool, page * sb_heads, sb_dh),
               cache_sb_v.reshape(n_pool, page * sb_heads, sb_dh), page_table, bias_cols, ut2, sb_g)

    memkv = _norm_proj(mem_prompt.reshape(mem_tok, d), row(mem_g), w_mem_b, tn=mix_w)
    if nb1:
        xp, o_sb_s = _ffn_sb(x_prompt.reshape(tp, d), *ffn1, *sb_args, heads=sb_heads, b0=0,
                             n_batches=nb1, pages_per_step=gp)
    else:
        xp, o_sb_s = _ffn(x_prompt.reshape(tp, d), *ffn1), None
    if nb1 < bs:
        rest = _sb_sample(*sb_args, heads=sb_heads, b0=nb1, n_batches=bs - nb1)
        o_sb_s = rest if o_sb_s is None else jnp.concatenate([o_sb_s, rest], axis=0)
    proj_p, qkv_p = _norm_proj(xp, row(mix_g_pre), w_in_b, tn=mix_w, n_bf16=3,
                               bf16_scale0=sb_dh ** -0.5)
    u_strict2 = _strict_upper_pair(V7X_MXU_DIM)
    u_incl = u_strict2[:V7X_MXU_DIM] + jnp.eye(V7X_MXU_DIM, dtype=BF16)
    o_sb_p = _sb_prompt(qkv_p, sb_bias.astype(F32), u_strict2, u_incl, sb_g, heads=sb_heads)
    o_rg_p, h_p = _rglru_prompt(proj_p, *rg_vecs, x_part=P_X, gate_part=P_GR)
    o_mem_p = _mem_attn_prompt(proj_p, memkv, q_part=P_QM, heads=mem_heads)
    y_p = _ffn(mix(xp, (o_sb_p, o_rg_p, o_mem_p), proj_p), *ffn2)

    to_tm = lambda a: a.reshape(bs, ts, mix_w).transpose(1, 0, 2)
    o_rg_tm, h_s = _rglru_sample(to_tm(proj_s[P_X]), state_conv.transpose(1, 0, 2),
                                 to_tm(proj_s[P_GR]), state_rglru, *rg_vecs, first_pos=past_len)
    o_rg_s = o_rg_tm.transpose(1, 0, 2)
    o_mem_s = _mem_attn_sample(proj_s4, cache_mem_k.reshape(bs, mem_tok, mix_w),
                               cache_mem_v.reshape(bs, mem_tok, mix_w), q_part=P_QM,
                               heads=mem_heads)
    branches_s = tuple(o.reshape(ms, mix_w) for o in (o_sb_s, o_rg_s, o_mem_s))
    y_s = _ffn(mix(xs, branches_s, proj_s), *ffn2)

    n_keep = conv_w.shape[0] - 1
    xr_s = proj_s[P_X].reshape(bs, ts, mix_w)
    conv_s = jnp.concatenate([state_conv.astype(F32), xr_s], axis=1)[:, ts:]
    conv_p = lax.slice(proj_p, (P_X, tp - n_keep, 0), (P_X + 1, tp, mix_w))
    heads4 = lambda a, b_, t_: a.reshape(b_, t_, sb_heads, sb_dh)
    return (
        y_p.reshape(bp, tp, d),
        y_s.reshape(bs, ts, d),
        heads4(proj_p[P_K], bp, tp), heads4(proj_p[P_V], bp, tp),
        heads4(proj_s[P_K], bs, ts), heads4(proj_s[P_V], bs, ts),
        conv_p, conv_s,
        h_p, h_s,
        memkv[0].reshape(bp, mem_tok, mem_heads, mix_w // mem_heads),
        memkv[1].reshape(bp, mem_tok, mem_heads, mix_w // mem_heads),
    )
```

```python
import functools
import math

import jax
import jax.numpy as jnp
from jax import lax
from jax.experimental import pallas as pl
from jax.experimental.pallas import tpu as pltpu

F32 = jnp.float32
BF16 = jnp.bfloat16

EPS = 1e-6
RG_C = 8.0
N_BRANCH = 3
SB_HEADS = 8
MEM_HEADS = 4
RG_BLOCKS = 8

V7X_LANES = 128
V7X_SUBLANES = 8
V7X_MXU_DIM = 256
V7X_VMEM_BYTES = 64 * 1024 * 1024
VMEM_LIMIT = V7X_VMEM_BYTES - 8 * 1024 * 1024

_LOG2E = 1.4426950408889634
_NT = (((1,), (1,)), ((), ()))
_TN = (((0,), (0,)), ((), ()))


def _params(*sem):
    return pltpu.CompilerParams(dimension_semantics=sem, vmem_limit_bytes=VMEM_LIMIT)


def _rms(x, g):
    ms = jnp.mean(x * x, axis=-1, keepdims=True)
    return x * lax.rsqrt(ms + EPS) * g


def _softplus(z):
    return jnp.maximum(z, 0.0) + jnp.log(1.0 + jnp.exp2(jnp.abs(z) * (-_LOG2E)))


def _split_bf16(x):
    hi = x.astype(BF16)
    lo = (x - hi.astype(F32)).astype(BF16)
    return hi, lo


def _row_block(m, target):
    return target if m % target == 0 else m


def _ffn_kernel(x_ref, gpre_ref, gpost_ref, wg_ref, wu_ref, wd_ref, o_ref, h_ref):
    j = pl.program_id(1)

    @pl.when(j == 0)
    def _():
        h_ref[...] = _rms(x_ref[...], gpre_ref[...]).astype(BF16)
        o_ref[...] = jnp.zeros_like(o_ref)

    h = h_ref[...]
    gate = jnp.dot(h, wg_ref[...], preferred_element_type=F32)
    up = jnp.dot(h, wu_ref[...], preferred_element_type=F32)
    act = (gate * jax.nn.sigmoid(gate) * up).astype(BF16)
    o_ref[...] += jnp.dot(act, wd_ref[...], preferred_element_type=F32)

    @pl.when(j == pl.num_programs(1) - 1)
    def _():
        o_ref[...] = x_ref[...] + 0.5 * _rms(o_ref[...], gpost_ref[...])


_FFN_TM = 512
_FFN_TF = 512
_PAGES_PER_STEP = 8


def _ffn(x, g_pre, g_post, w_gu, w_down, *, tm=_FFN_TM, tf=_FFN_TF):
    m, d = x.shape
    f = w_down.shape[0]
    tm = _row_block(m, tm)
    nf = f // tf
    return pl.pallas_call(
        _ffn_kernel,
        grid=(m // tm, nf),
        in_specs=[
            pl.BlockSpec((tm, d), lambda i, j: (i, 0)),
            pl.BlockSpec((1, d), lambda i, j: (0, 0)),
            pl.BlockSpec((1, d), lambda i, j: (0, 0)),
            pl.BlockSpec((d, tf), lambda i, j: (0, j)),
            pl.BlockSpec((d, tf), lambda i, j: (0, j + nf)),
            pl.BlockSpec((tf, d), lambda i, j: (j, 0)),
        ],
        out_specs=pl.BlockSpec((tm, d), lambda i, j: (i, 0)),
        out_shape=jax.ShapeDtypeStruct((m, d), F32),
        scratch_shapes=[pltpu.VMEM((tm, d), BF16)],
        compiler_params=_params("parallel", "arbitrary"),
        name="ffn",
    )(x, g_pre, g_post, w_gu, w_gu, w_down)


def _norm_proj_kernel(x_ref, g_ref, w_ref, o_ref, *rest, n_bf16, bf16_scale0):
    if n_bf16:
        obf_ref, h_ref = rest
    else:
        (h_ref,) = rest
    j = pl.program_id(1)

    @pl.when(j == 0)
    def _():
        h_ref[...] = _rms(x_ref[...], g_ref[...]).astype(BF16)

    y = jnp.dot(h_ref[...], w_ref[...], preferred_element_type=F32)
    o_ref[...] = y
    if n_bf16:
        @pl.when(j < n_bf16)
        def _():
            obf_ref[...] = (y * jnp.where(j == 0, bf16_scale0, 1.0)).astype(BF16)


def _norm_proj(x, g, w, *, tn, n_bf16=0, bf16_scale0=1.0, tm=1024):
    m, d = x.shape
    n = w.shape[1]
    tm = _row_block(m, tm)
    parts = n // tn
    out_shape = [jax.ShapeDtypeStruct((parts, m, tn), F32)]
    out_specs = [pl.BlockSpec((None, tm, tn), lambda i, j: (j, i, 0))]
    if n_bf16:
        out_shape.append(jax.ShapeDtypeStruct((n_bf16, m, tn), BF16))
        out_specs.append(
            pl.BlockSpec((None, tm, tn), lambda i, j: (jnp.minimum(j, n_bf16 - 1), i, 0)))
    res = pl.pallas_call(
        functools.partial(_norm_proj_kernel, n_bf16=n_bf16, bf16_scale0=bf16_scale0),
        grid=(m // tm, parts),
        in_specs=[
            pl.BlockSpec((tm, d), lambda i, j: (i, 0)),
            pl.BlockSpec((1, d), lambda i, j: (0, 0)),
            pl.BlockSpec((d, tn), lambda i, j: (0, j)),
        ],
        out_specs=out_specs,
        out_shape=out_shape,
        scratch_shapes=[pltpu.VMEM((tm, d), BF16)],
        compiler_params=_params("parallel", "arbitrary"),
        name="norm_proj",
    )(x, g, w)
    return res if n_bf16 else res[0]


_BIAS_TERMS = 3


def _sb_prompt_kernel(bias_ref, q_ref, k_ref, v_ref, u2_ref, ui_ref, g_ref, o_ref, acc_ref, c_ref,
                      kaug_ref, *, bq, bk, fill_rows):
    h = pl.program_id(0)
    i = pl.program_id(1)
    bias = bias_ref[h]
    r = bq // bk
    t, dh = k_ref.shape

    @pl.when(i == 0)
    def _():
        lane = lax.broadcasted_iota(jnp.int32, (fill_rows, dh), 1)
        rem = jnp.full((fill_rows, dh), bias, F32)
        cols = jnp.zeros((fill_rows, dh), F32)
        for n in range(_BIAS_TERMS):
            term = rem.astype(BF16).astype(F32)
            cols = jnp.where(lane == n, term, cols)
            rem = rem - term
        cols = cols.astype(BF16)
        for start in range(0, t, fill_rows):
            kaug_ref[start:start + fill_rows, 0:dh] = k_ref[start:start + fill_rows, :]
            kaug_ref[start:start + fill_rows, dh:2 * dh] = cols

    ones = jnp.where(lax.broadcasted_iota(jnp.int32, (bq, dh), 1) < _BIAS_TERMS, 1.0, 0.0)
    q = jnp.concatenate([q_ref[...], ones.astype(BF16)], axis=1)
    sp0 = _softplus(jnp.full((1, bk), bias, F32))
    col_const = sp0 * (bk - lax.broadcasted_iota(jnp.int32, (1, bk), 1)).astype(F32)

    def scores(kb, q_rows):
        start = pl.multiple_of(kb * bk, bk)
        return lax.dot_general(q_rows, kaug_ref[pl.ds(start, bk), :], _NT,
                               preferred_element_type=F32), v_ref[pl.ds(start, bk), :]

    def diag_tile(kb, q_rows, c):
        z, vblk = scores(kb, q_rows)
        keep = (lax.broadcasted_iota(jnp.int32, (bk, bk), 1)
                < lax.broadcasted_iota(jnp.int32, (bk, bk), 0))
        sp = jnp.where(keep, _softplus(z), 0.0)
        hi, lo = _split_bf16(sp)
        excl = jnp.dot(jnp.concatenate([hi, lo], axis=1), u2_ref[...], preferred_element_type=F32)
        w = jnp.where(keep, jnp.exp(z - sp - excl - c), 0.0)
        pv = jnp.dot(w.astype(BF16), vblk, preferred_element_type=F32)
        return pv, jnp.sum(sp, axis=1, keepdims=True)

    def plain_block(kb, q_rows, c):
        z, vblk = scores(kb, q_rows)
        d = (_softplus(z) - sp0).astype(BF16)
        incl = jnp.dot(d, ui_ref[...], preferred_element_type=F32)
        w = jnp.exp(z - incl - col_const - c)
        pv = jnp.dot(w.astype(BF16), vblk, preferred_element_type=F32)
        return pv, incl[:, 0:1] + col_const[:, 0:1]

    acc_ref[...] = jnp.zeros_like(acc_ref)
    c_ref[...] = jnp.zeros_like(c_ref)
    for sub in reversed(range(r)):
        r0, r1 = sub * bk, (sub + 1) * bk
        pv, tot = diag_tile(i * r + sub, q[r0:r1, :], c_ref[r0:r1, :])
        acc_ref[r0:r1, :] += pv
        c_ref[r0:r1, :] += tot
        if r1 < bq:
            pv, tot = plain_block(i * r + sub, q[r1:, :], c_ref[r1:, :])
            acc_ref[r1:, :] += pv
            c_ref[r1:, :] += tot

    def body(n, carry):
        c = c_ref[...]
        pvs = None
        for d in range(r):
            pv, tot = plain_block((i - n) * r - 1 - d, q, c)
            pvs = pv if pvs is None else pvs + pv
            c = c + tot
        acc_ref[...] += pvs
        c_ref[...] = c
        return carry

    lax.fori_loop(0, i, body, 0)
    o_ref[...] = _rms(acc_ref[...], g_ref[...]).astype(BF16)


def _sb_prompt(qkv, bias, u_strict2, u_incl, g, *, heads, bq=2048, bk=V7X_MXU_DIM):
    _, t, w = qkv.shape
    dh = w // heads
    bq = min(bq, t)
    fill_rows = min(t, 2048)
    assert t % bq == 0 and bq % bk == 0 and t % fill_rows == 0
    return pl.pallas_call(
        functools.partial(_sb_prompt_kernel, bq=bq, bk=bk, fill_rows=fill_rows),
        grid=(heads, t // bq),
        in_specs=[
            pl.BlockSpec(memory_space=pltpu.SMEM),
            pl.BlockSpec((None, bq, dh), lambda h, i: (0, i, h)),
            pl.BlockSpec((None, t, dh), lambda h, i: (1, 0, h)),
            pl.BlockSpec((None, t, dh), lambda h, i: (2, 0, h)),
            pl.BlockSpec((2 * bk, bk), lambda h, i: (0, 0)),
            pl.BlockSpec((bk, bk), lambda h, i: (0, 0)),
            pl.BlockSpec((1, dh), lambda h, i: (0, 0)),
        ],
        out_specs=pl.BlockSpec((bq, dh), lambda h, i: (i, h)),
        out_shape=jax.ShapeDtypeStruct((t, w), BF16),
        scratch_shapes=[pltpu.VMEM((bq, dh), F32), pltpu.VMEM((bq, 1), F32),
                        pltpu.VMEM((t, 2 * dh), BF16)],
        compiler_params=_params("arbitrary", "arbitrary"),
        name="sb_prompt",
    )(bias, qkv, qkv, qkv, u_strict2, u_incl, g)


_QPAD = 8


def _sb_sample_phases(q_ref, kn_ref, vn_ref, bias_ref, ut_ref, g_ref, kv_refs, o_ref, qrows_ref,
                      acc_ref, c_ref, kpad_ref, vpad_ref, *, heads, scale):
    pages_per_step = len(kv_refs) // 2
    ts, w = q_ref.shape
    dh = w // heads
    ncol = heads * _QPAD
    page = kpad_ref.shape[0]

    def by_key(ref):
        parts = [ref[pl.ds(hh, page, stride=heads), :] for hh in range(heads)]
        return jnp.concatenate(parts, axis=1).astype(BF16)

    def log_weights(kbs, keep=None):
        zs = [lax.dot_general(kb, qrows_ref[...], _NT, preferred_element_type=F32) * scale
              + bias_ref[...] for kb in kbs]
        sps = [_softplus(z) for z in zs]
        if keep is not None:
            sps = [jnp.where(keep, sp, 0.0) for sp in sps]
        excls = [jnp.dot(ut_ref[...], jnp.concatenate(_split_bf16(sp), axis=0),
                         preferred_element_type=F32) for sp in sps]
        return ([z - sp - excl for z, sp, excl in zip(zs, sps, excls)],
                [jnp.sum(sp, axis=0, keepdims=True) for sp in sps])

    def start():
        q8 = jnp.concatenate([q_ref[...], jnp.zeros((_QPAD - ts, w), F32)], axis=0)
        qt = jnp.concatenate([q8] * heads, axis=0)
        row_head = lax.broadcasted_iota(jnp.int32, (ncol, w), 0) // _QPAD
        col_head = lax.broadcasted_iota(jnp.int32, (ncol, w), 1) // dh
        qrows_ref[...] = jnp.where(row_head == col_head, qt, 0.0).astype(BF16)
        kpad_ref[...] = jnp.zeros_like(kpad_ref)
        vpad_ref[...] = jnp.zeros_like(vpad_ref)
        kpad_ref[0:ts, :] = kn_ref[...]
        vpad_ref[0:ts, :] = vn_ref[...]
        key = lax.broadcasted_iota(jnp.int32, (page, ncol), 0)
        qry = lax.broadcasted_iota(jnp.int32, (page, ncol), 1) % _QPAD
        keep = (key < qry) & (qry < ts)
        (arg,), (tot,) = log_weights([kpad_ref[...].astype(BF16)], keep)
        wgt = jnp.where(keep, jnp.exp(arg), 0.0).astype(BF16)
        acc_ref[...] = lax.dot_general(wgt, vpad_ref[...].astype(BF16), _TN,
                                       preferred_element_type=F32)
        c_ref[...] = tot

    def page_scores():
        return [lax.dot_general(by_key(kv_refs[r]), qrows_ref[...], _NT,
                                preferred_element_type=F32) * scale + bias_ref[...]
                for r in range(pages_per_step)]

    def page_cumsums(zs):
        sps = [_softplus(z) for z in zs]
        excls = [jnp.dot(ut_ref[...], jnp.concatenate(_split_bf16(sp), axis=0),
                         preferred_element_type=F32) for sp in sps]
        return ([z - sp - excl for z, sp, excl in zip(zs, sps, excls)],
                [jnp.sum(sp, axis=0, keepdims=True) for sp in sps])

    def page_sum(args_tots, valid):
        c = c_ref[...]
        wgts = []
        for arg, tot in zip(*args_tots):
            wgts.append((jnp.exp(arg - c) * valid).astype(BF16))
            c = c + tot * valid
        c_ref[...] = c
        vals = [by_key(kv_refs[pages_per_step + r]) for r in range(pages_per_step)]
        acc_ref[...] += lax.dot_general(jnp.concatenate(wgts, axis=0),
                                        jnp.concatenate(vals, axis=0), _TN,
                                        preferred_element_type=F32)

    def finish():
        outs = []
        for hh in range(heads):
            blk = acc_ref[hh * _QPAD:hh * _QPAD + ts, hh * dh:(hh + 1) * dh]
            outs.append(_rms(blk, g_ref[...]))
        o_ref[...] = jnp.concatenate(outs, axis=1).astype(BF16)

    return start, (page_scores, page_cumsums, page_sum), finish


def _sb_sample_kernel(pt_ref, q_ref, kn_ref, vn_ref, bias_ref, ut_ref, g_ref, *rest,
                      pages_per_step, heads, scale):
    kv_refs = rest[:2 * pages_per_step]
    o_ref, qrows_ref, acc_ref, c_ref, kpad_ref, vpad_ref = rest[2 * pages_per_step:]
    del pt_ref
    s = pl.program_id(1)
    start, (page_scores, page_cumsums, page_sum), finish = _sb_sample_phases(
        q_ref, kn_ref, vn_ref, bias_ref, ut_ref, g_ref, kv_refs, o_ref, qrows_ref, acc_ref,
        c_ref, kpad_ref, vpad_ref, heads=heads, scale=scale)
    pl.when(s == 0)(start)
    page_sum(page_cumsums(page_scores()), 1.0)
    pl.when(s == pl.num_programs(1) - 1)(finish)


def _sb_sample(proj, cache_k, cache_v, page_table, bias_cols, ut2, g, *, heads, b0, n_batches,
               pages_per_step=16):
    _, _, ts, w = proj.shape
    dh = w // heads
    page = cache_k.shape[1] // heads
    n_pages = page_table.shape[1]
    assert ts <= _QPAD and dh == V7X_LANES and cache_k.shape[2] == dh
    gp = pages_per_step if n_pages % pages_per_step == 0 else 1
    ncol = heads * _QPAD

    def page_spec(r):
        return pl.BlockSpec(
            (None, page * heads, dh),
            lambda bi, s, pt: (pt[b0 + bi, n_pages - 1 - (s * gp + r)], 0, 0))

    def tok_spec(p):
        return pl.BlockSpec((None, None, ts, w), lambda bi, s, pt: (p, b0 + bi, 0, 0))

    grid_spec = pltpu.PrefetchScalarGridSpec(
        num_scalar_prefetch=1,
        grid=(n_batches, n_pages // gp),
        in_specs=[
            tok_spec(0), tok_spec(1), tok_spec(2),
            pl.BlockSpec((1, ncol), lambda bi, s, pt: (0, 0)),
            pl.BlockSpec((page, 2 * page), lambda bi, s, pt: (0, 0)),
            pl.BlockSpec((1, dh), lambda bi, s, pt: (0, 0)),
        ] + [page_spec(r) for r in range(gp)] * 2,
        out_specs=pl.BlockSpec((None, ts, w), lambda bi, s, pt: (bi, 0, 0)),
        scratch_shapes=[
            pltpu.VMEM((ncol, w), BF16),
            pltpu.VMEM((ncol, w), F32),
            pltpu.VMEM((1, ncol), F32),
            pltpu.VMEM((page, w), F32),
            pltpu.VMEM((page, w), F32),
        ],
    )
    return pl.pallas_call(
        functools.partial(_sb_sample_kernel, pages_per_step=gp, heads=heads, scale=dh ** -0.5),
        grid_spec=grid_spec,
        out_shape=jax.ShapeDtypeStruct((n_batches, ts, w), BF16),
        compiler_params=_params("arbitrary", "arbitrary"),
        name="sb_sample",
    )(page_table, proj, proj, proj, bias_cols, ut2, g,
      *([cache_k] * gp), *([cache_v] * gp))


def _ffn_sb_kernel(pt_ref, x_ref, gpre_ref, gpost_ref, wg_ref, wu_ref, wd_ref,
                   q_ref, kn_ref, vn_ref, bias_ref, ut_ref, g_ref, *rest,
                   pages_per_step, heads, scale, steps_per_batch, n_batches, skip_idle):
    kv_refs = rest[:2 * pages_per_step]
    o_ref, osb_ref, h_ref, qrows_ref, acc_ref, c_ref, kpad_ref, vpad_ref = rest[2 * pages_per_step:]
    del pt_ref
    j = pl.program_id(1)
    last_j = pl.num_programs(1) - 1
    step = pl.program_id(0) * pl.num_programs(1) + j
    n_valid = n_batches * steps_per_batch
    valid = step < n_valid
    s = jnp.minimum(step, n_valid - 1) % steps_per_batch
    start, (page_scores, page_cumsums, page_sum), finish = _sb_sample_phases(
        q_ref, kn_ref, vn_ref, bias_ref, ut_ref, g_ref, kv_refs, osb_ref, qrows_ref, acc_ref,
        c_ref, kpad_ref, vpad_ref, heads=heads, scale=scale)

    @pl.when(j == 0)
    def _():
        h_ref[...] = _rms(x_ref[...], gpre_ref[...]).astype(BF16)
        o_ref[...] = jnp.zeros_like(o_ref)

    pl.when(valid & (s == 0))(start)

    h = h_ref[...]
    if not skip_idle:
        zs = page_scores()
        gate = jnp.dot(h, wg_ref[...], preferred_element_type=F32)
        up = jnp.dot(h, wu_ref[...], preferred_element_type=F32)
        args_tots = page_cumsums(zs)
        act = (gate * jax.nn.sigmoid(gate) * up).astype(BF16)
        o_ref[...] += jnp.dot(act, wd_ref[...], preferred_element_type=F32)
        page_sum(args_tots, jnp.where(valid, 1.0, 0.0))
    else:
        gate = jnp.dot(h, wg_ref[...], preferred_element_type=F32)
        up = jnp.dot(h, wu_ref[...], preferred_element_type=F32)
        act = (gate * jax.nn.sigmoid(gate) * up).astype(BF16)
        o_ref[...] += jnp.dot(act, wd_ref[...], preferred_element_type=F32)

        @pl.when(valid)
        def _():
            page_sum(page_cumsums(page_scores()), 1.0)

    @pl.when(j == last_j)
    def _():
        o_ref[...] = x_ref[...] + 0.5 * _rms(o_ref[...], gpost_ref[...])

    pl.when(valid & (s == steps_per_batch - 1))(finish)


def _ffn_sb(x, g_pre, g_post, w_gu, w_down, proj, cache_k, cache_v, page_table, bias_cols, ut2, g,
            *, heads, b0, n_batches, pages_per_step, tm=_FFN_TM, tf=_FFN_TF):
    m, d = x.shape
    f = w_down.shape[0]
    tm = _row_block(m, tm)
    nf = f // tf
    _, _, ts, w = proj.shape
    dh = w // heads
    page = cache_k.shape[1] // heads
    n_pages = page_table.shape[1]
    gp = pages_per_step
    spb = n_pages // gp
    ncol = heads * _QPAD
    assert ts <= _QPAD and dh == V7X_LANES and cache_k.shape[2] == dh and n_pages % gp == 0
    assert 0 < n_batches * spb <= (m // tm) * nf

    n_steps = (m // tm) * nf
    st = jnp.minimum(jnp.arange(n_steps, dtype=jnp.int32), n_batches * spb - 1)
    rel_b, s_in_b = st // spb, st % spb
    slots = n_pages - 1 - (s_in_b[:, None] * gp + jnp.arange(gp, dtype=jnp.int32)[None, :])
    page_ids = page_table[b0 + rel_b[:, None], slots].astype(jnp.int32)
    sched = jnp.concatenate([rel_b[:, None], page_ids], axis=1).reshape(-1)
    width = gp + 1

    def page_spec(r):
        return pl.BlockSpec((None, page * heads, dh),
                            lambda i, j, sc: (sc[(i * nf + j) * width + 1 + r], 0, 0))

    def tok_spec(p):
        return pl.BlockSpec((None, None, ts, w),
                            lambda i, j, sc: (p, b0 + sc[(i * nf + j) * width], 0, 0))

    grid_spec = pltpu.PrefetchScalarGridSpec(
        num_scalar_prefetch=1,
        grid=(m // tm, nf),
        in_specs=[
            pl.BlockSpec((tm, d), lambda i, j, pt: (i, 0)),
            pl.BlockSpec((1, d), lambda i, j, pt: (0, 0)),
            pl.BlockSpec((1, d), lambda i, j, pt: (0, 0)),
            pl.BlockSpec((d, tf), lambda i, j, pt: (0, j)),
            pl.BlockSpec((d, tf), lambda i, j, pt: (0, j + nf)),
            pl.BlockSpec((tf, d), lambda i, j, pt: (j, 0)),
            tok_spec(0), tok_spec(1), tok_spec(2),
            pl.BlockSpec((1, ncol), lambda i, j, pt: (0, 0)),
            pl.BlockSpec((page, 2 * page), lambda i, j, pt: (0, 0)),
            pl.BlockSpec((1, dh), lambda i, j, pt: (0, 0)),
        ] + [page_spec(r) for r in range(gp)] * 2,
        out_specs=[
            pl.BlockSpec((tm, d), lambda i, j, sc: (i, 0)),
            pl.BlockSpec((None, ts, w), lambda i, j, sc: (sc[(i * nf + j) * width], 0, 0)),
        ],
        scratch_shapes=[
            pltpu.VMEM((tm, d), BF16),
            pltpu.VMEM((ncol, w), BF16),
            pltpu.VMEM((ncol, w), F32),
            pltpu.VMEM((1, ncol), F32),
            pltpu.VMEM((page, w), F32),
            pltpu.VMEM((page, w), F32),
        ],
    )
    return pl.pallas_call(
        functools.partial(_ffn_sb_kernel, pages_per_step=gp, heads=heads, scale=dh ** -0.5,
                          steps_per_batch=spb, n_batches=n_batches,
                          skip_idle=2 * n_batches * spb < n_steps),
        grid_spec=grid_spec,
        out_shape=[jax.ShapeDtypeStruct((m, d), F32),
                   jax.ShapeDtypeStruct((n_batches, ts, w), BF16)],
        compiler_params=_params("arbitrary", "arbitrary"),
        name="ffn_sb",
    )(sched, x, g_pre, g_post, w_gu, w_gu, w_down, proj, proj, proj, bias_cols, ut2, g,
      *([cache_k] * gp), *([cache_v] * gp))


def _gelu_tanh(x):
    return 0.5 * x * (1.0 + jnp.tanh(math.sqrt(2.0 / math.pi) * (x + 0.044715 * (x * x * x))))


def _rg_coeffs(xc, wa_ref, ba, wx_ref, bx, lam, reset=None):
    blocks = wa_ref.shape[0]
    bw = wa_ref.shape[1]
    ra, rx = [], []
    for n in range(blocks):
        xb = xc[:, n * bw:(n + 1) * bw].astype(BF16)
        ra.append(jnp.dot(xb, wa_ref[n], preferred_element_type=F32))
        rx.append(jnp.dot(xb, wx_ref[n], preferred_element_type=F32))
    r = jax.nn.sigmoid(jnp.concatenate(ra, axis=1) + ba)
    ig = jax.nn.sigmoid(jnp.concatenate(rx, axis=1) + bx)
    log_a = -RG_C * r * _softplus(-lam)
    a = jnp.exp(log_a)
    y = 2.0 * log_a
    u = jnp.exp(y)
    mid = (u < 1.0) & (u > 0.0)
    ratio = (1.0 - u) * y / jnp.log(jnp.where(mid, u, 0.5))
    neg_expm1 = jnp.where(mid, ratio, jnp.where(u > 0.0, -y, 1.0))
    mult = jnp.sqrt(neg_expm1)
    if reset is not None:
        a = jnp.where(reset, 0.0, a)
        mult = jnp.where(reset, 1.0, mult)
    return a, mult * ig * xc


def _rglru_prompt_kernel(x_ref, gr_ref, cw_ref, cb_ref, wa_ref, ba_ref, wx_ref, bx_ref, lam_ref,
                         o_ref, hlast_ref, xbuf_ref, a_ref, b_ref, hs_ref, h_ref, *, taps):
    step = pl.program_id(0)
    tc = x_ref.shape[0]
    pad = V7X_SUBLANES

    @pl.when(step == 0)
    def _():
        xbuf_ref[0:pad, :] = jnp.zeros((pad, x_ref.shape[1]), F32)
        h_ref[...] = jnp.zeros_like(h_ref)

    xbuf_ref[pad:pad + tc, :] = x_ref[...]
    base = pad - (taps - 1)
    xc = cb_ref[...] + xbuf_ref[base:base + tc, :] * cw_ref[0:1, :]
    for tap in range(1, taps):
        xc = xc + xbuf_ref[base + tap:base + tap + tc, :] * cw_ref[tap:tap + 1, :]
    reset = (step * tc + lax.broadcasted_iota(jnp.int32, xc.shape, 0)) == 0
    a, b = _rg_coeffs(xc, wa_ref, ba_ref[...], wx_ref, bx_ref[...], lam_ref[...], reset)
    a_ref[...] = a
    b_ref[...] = b

    def body(t, h):
        h = a_ref[pl.ds(t, 1), :] * h + b_ref[pl.ds(t, 1), :]
        hs_ref[pl.ds(t, 1), :] = h
        return h

    h = lax.fori_loop(0, tc, body, h_ref[...], unroll=8)
    h_ref[...] = h
    hlast_ref[...] = h
    o_ref[...] = (_gelu_tanh(gr_ref[...]) * hs_ref[...]).astype(BF16)
    xbuf_ref[0:pad, :] = xbuf_ref[tc:tc + pad, :]


def _rglru_prompt(proj, conv_w, conv_b, wa, ba, wx, bx, lam, *, x_part, gate_part, tc=512):
    _, t, w = proj.shape
    taps = conv_w.shape[0]
    tc = _row_block(t, tc)
    assert tc % V7X_SUBLANES == 0 and taps - 1 <= V7X_SUBLANES
    const2 = lambda i: (0, 0)
    const3 = lambda i: (0, 0, 0)
    return pl.pallas_call(
        functools.partial(_rglru_prompt_kernel, taps=taps),
        grid=(t // tc,),
        in_specs=[
            pl.BlockSpec((None, tc, w), lambda i: (x_part, i, 0)),
            pl.BlockSpec((None, tc, w), lambda i: (gate_part, i, 0)),
            pl.BlockSpec(conv_w.shape, const2),
            pl.BlockSpec((1, w), const2),
            pl.BlockSpec(wa.shape, const3),
            pl.BlockSpec((1, w), const2),
            pl.BlockSpec(wx.shape, const3),
            pl.BlockSpec((1, w), const2),
            pl.BlockSpec((1, w), const2),
        ],
        out_specs=[pl.BlockSpec((tc, w), lambda i: (i, 0)), pl.BlockSpec((1, w), const2)],
        out_shape=[jax.ShapeDtypeStruct((t, w), BF16), jax.ShapeDtypeStruct((1, w), F32)],
        scratch_shapes=[
            pltpu.VMEM((tc + V7X_SUBLANES, w), F32),
            pltpu.VMEM((tc, w), F32),
            pltpu.VMEM((tc, w), F32),
            pltpu.VMEM((tc, w), F32),
            pltpu.VMEM((1, w), F32),
        ],
        compiler_params=_params("arbitrary"),
        name="rglru_prompt",
    )(proj, proj, conv_w, conv_b, wa, ba, wx, bx, lam)


def _rglru_sample_kernel(x_ref, buf_ref, gr_ref, h0_ref, cw_ref, cb_ref, wa_ref, ba_ref, wx_ref,
                         bx_ref, lam_ref, o_ref, hnew_ref, *, first_pos):
    ts = x_ref.shape[0]
    taps = cw_ref.shape[0]
    rows = [buf_ref[n] for n in range(taps - 1)] + [x_ref[n] for n in range(ts)]
    h = h0_ref[...]
    for t in range(ts):
        xc = cb_ref[...] + rows[t] * cw_ref[0:1, :]
        for tap in range(1, taps):
            xc = xc + rows[t + tap] * cw_ref[tap:tap + 1, :]
        reset = jnp.full(xc.shape, True) if first_pos + t == 0 else None
        a, b = _rg_coeffs(xc, wa_ref, ba_ref[...], wx_ref, bx_ref[...], lam_ref[...], reset)
        h = a * h + b
        o_ref[t] = (_gelu_tanh(gr_ref[t]) * h).astype(BF16)
    hnew_ref[...] = h


def _rglru_sample(x_tm, buf_tm, gr_tm, h0, conv_w, conv_b, wa, ba, wx, bx, lam, *, first_pos):
    ts, b, w = x_tm.shape
    return pl.pallas_call(
        functools.partial(_rglru_sample_kernel, first_pos=first_pos),
        out_shape=[jax.ShapeDtypeStruct((ts, b, w), BF16), jax.ShapeDtypeStruct((b, w), F32)],
        compiler_params=pltpu.CompilerParams(vmem_limit_bytes=VMEM_LIMIT),
        name="rglru_sample",
    )(x_tm, buf_tm, gr_tm, h0, conv_w, conv_b, wa, ba, wx, bx, lam)


def _mem_attn_kernel(q_ref, mk_ref, mv_ref, o_ref, *, heads):
    w = q_ref.shape[-1]
    dh = w // heads
    outs = []
    for hh in range(heads):
        sl = slice(hh * dh, (hh + 1) * dh)
        qh = q_ref[:, sl].astype(BF16)
        kh = mk_ref[:, sl].astype(BF16)
        vh = mv_ref[:, sl].astype(BF16)
        s = lax.dot_general(qh, kh, _NT, preferred_element_type=F32) * (dh ** -0.5)
        e = jnp.exp(s - jnp.max(s, axis=-1, keepdims=True))
        p = e / jnp.sum(e, axis=-1, keepdims=True)
        outs.append(jnp.dot(p.astype(BF16), vh, preferred_element_type=F32))
    o_ref[...] = jnp.concatenate(outs, axis=1).astype(BF16)


def _mem_attn_prompt(proj, memkv, *, q_part, heads, tm=512):
    _, t, w = proj.shape
    mtok = memkv.shape[1]
    tm = _row_block(t, tm)
    return pl.pallas_call(
        functools.partial(_mem_attn_kernel, heads=heads),
        grid=(t // tm,),
        in_specs=[
            pl.BlockSpec((None, tm, w), lambda i: (q_part, i, 0)),
            pl.BlockSpec((None, mtok, w), lambda i: (0, 0, 0)),
            pl.BlockSpec((None, mtok, w), lambda i: (1, 0, 0)),
        ],
        out_specs=pl.BlockSpec((tm, w), lambda i: (i, 0)),
        out_shape=jax.ShapeDtypeStruct((t, w), BF16),
        compiler_params=_params("parallel"),
        name="mem_attn_prompt",
    )(proj, memkv, memkv)


def _mem_attn_sample(proj, mem_k, mem_v, *, q_part, heads):
    _, b, ts, w = proj.shape
    mtok = mem_k.shape[1]
    return pl.pallas_call(
        functools.partial(_mem_attn_kernel, heads=heads),
        grid=(b,),
        in_specs=[
            pl.BlockSpec((None, None, ts, w), lambda i: (q_part, i, 0, 0)),
            pl.BlockSpec((None, mtok, w), lambda i: (i, 0, 0)),
            pl.BlockSpec((None, mtok, w), lambda i: (i, 0, 0)),
        ],
        out_specs=pl.BlockSpec((None, ts, w), lambda i: (i, 0, 0)),
        out_shape=jax.ShapeDtypeStruct((b, ts, w), BF16),
        compiler_params=_params("parallel"),
        name="mem_attn_sample",
    )(proj, mem_k, mem_v)


def _merge_kernel(*refs, n_branch, gate_parts):
    o_refs = refs[:n_branch]
    g_refs = refs[n_branch:n_branch + n_branch * gate_parts]
    bg_ref, wb_ref, m_ref = refs[n_branch + n_branch * gate_parts:]
    w = g_refs[0].shape[1]
    for part in range(gate_parts):
        cols = slice(part * w, (part + 1) * w)
        acc = None
        for n in range(n_branch):
            gate = jax.nn.sigmoid(g_refs[n * gate_parts + part][...] + bg_ref[n][:, cols])
            term = gate * jnp.dot(o_refs[n][...], wb_ref[n, :, cols], preferred_element_type=F32)
            acc = term if acc is None else acc + term
        m_ref[:, cols] = acc.astype(BF16)


def _merge(branches, proj, b_gate3, w_branch, *, gate_part0, tm=256):
    m, w = branches[0].shape
    n_branch, _, d = w_branch.shape
    tm = _row_block(m, tm)
    gate_parts = d // w

    def gate_spec(p):
        return pl.BlockSpec((None, tm, w), lambda i: (gate_part0 + p, i, 0))

    resident = dict(pipeline_mode=pl.Buffered(1))
    return pl.pallas_call(
        functools.partial(_merge_kernel, n_branch=n_branch, gate_parts=gate_parts),
        grid=(m // tm,),
        in_specs=[pl.BlockSpec((tm, w), lambda i: (i, 0))] * n_branch
        + [gate_spec(p) for p in range(n_branch * gate_parts)]
        + [pl.BlockSpec((n_branch, 1, d), lambda i: (0, 0, 0), **resident),
           pl.BlockSpec((n_branch, w, d), lambda i: (0, 0, 0), **resident)],
        out_specs=pl.BlockSpec((tm, d), lambda i: (i, 0)),
        out_shape=jax.ShapeDtypeStruct((m, d), BF16),
        compiler_params=_params("parallel"),
        name="merge",
    )(*branches, *([proj] * (n_branch * gate_parts)), b_gate3, w_branch)


def _out_proj_kernel(x_ref, m_ref, w_ref, g_ref, o_ref):
    y = jnp.dot(m_ref[...], w_ref[...], preferred_element_type=F32)
    o_ref[...] = x_ref[...] + _rms(y, g_ref[...])


def _out_proj(x, merged, w_out, g, *, tm=512):
    m, d = x.shape
    tm = _row_block(m, tm)
    return pl.pallas_call(
        _out_proj_kernel,
        grid=(m // tm,),
        in_specs=[
            pl.BlockSpec((tm, d), lambda i: (i, 0)),
            pl.BlockSpec((tm, d), lambda i: (i, 0)),
            pl.BlockSpec((d, d), lambda i: (0, 0)),
            pl.BlockSpec((1, d), lambda i: (0, 0)),
        ],
        out_specs=pl.BlockSpec((tm, d), lambda i: (i, 0)),
        out_shape=jax.ShapeDtypeStruct((m, d), F32),
        compiler_params=_params("parallel"),
        name="out_proj",
    )(x, merged, w_out, g)


def _strict_upper_pair(n):
    j = lax.broadcasted_iota(jnp.int32, (n, n), 0)
    s = lax.broadcasted_iota(jnp.int32, (n, n), 1)
    u = (j > s).astype(BF16)
    return jnp.concatenate([u, u], axis=0)


def kernel(x_prompt, x_sample, cache_sb_k, cache_sb_v, state_conv, state_rglru, cache_mem_k, cache_mem_v, page_table, mem_prompt, ffn1_g_pre, ffn1_g_post, ffn1_w_gu, ffn1_w_down, mix_g_pre, mix_g_post, w_in, b_gate, sb_bias, sb_norm_g, conv_w, conv_b, rg_w_a, rg_b_a, rg_w_x, rg_b_x, rg_lambda, mem_g, w_mem_kv, w_branch, w_out, ffn2_g_pre, ffn2_g_post, ffn2_w_gu, ffn2_w_down):
    bp, tp, d = x_prompt.shape
    bs, ts, _ = x_sample.shape
    assert bp == 1
    n_pool, page, sb_heads, sb_dh = cache_sb_k.shape
    mix_w = sb_heads * sb_dh
    n_pages = page_table.shape[1]
    past_len = n_pages * page
    mem_tok, mem_heads = cache_mem_k.shape[1], cache_mem_k.shape[2]

    row = lambda v: v.reshape(1, -1).astype(F32)
    bf = lambda v: v.astype(BF16)
    w1_gu, w1_down = bf(ffn1_w_gu), bf(ffn1_w_down)
    w2_gu, w2_down = bf(ffn2_w_gu), bf(ffn2_w_down)
    w_in_b, w_mem_b = bf(w_in), bf(w_mem_kv)
    w_branch_b, w_out_b = bf(w_branch), bf(w_out)
    wa_b, wx_b = bf(rg_w_a), bf(rg_w_x)
    b_gate3 = b_gate.reshape(N_BRANCH, 1, d)
    sb_g = row(sb_norm_g)
    rg_vecs = (conv_w, row(conv_b), wa_b, row(rg_b_a), wx_b, row(rg_b_x), row(rg_lambda))
    P_Q, P_K, P_V, P_X, P_GR, P_QM, P_GL = range(7)

    ffn1 = (row(ffn1_g_pre), row(ffn1_g_post), w1_gu, w1_down)
    ffn2 = (row(ffn2_g_pre), row(ffn2_g_post), w2_gu, w2_down)

    def mix(x, branches, proj):
        merged = _merge(branches, proj, b_gate3, w_branch_b, gate_part0=P_GL)
        return _out_proj(x, merged, w_out_b, row(mix_g_post))

    ms = bs * ts
    xs = _ffn(x_sample.reshape(ms, d), *ffn1)
    proj_s = _norm_proj(xs, row(mix_g_pre), w_in_b, tn=mix_w)
    proj_s4 = proj_s.reshape(proj_s.shape[0], bs, ts, mix_w)
    ut = _strict_upper_pair(page)
    ut2 = jnp.concatenate([ut[:page].T, ut[:page].T], axis=1)
    bias_cols = jnp.repeat(sb_bias.astype(F32), _QPAD).reshape(1, sb_heads * _QPAD)
    gp = _PAGES_PER_STEP if n_pages % _PAGES_PER_STEP == 0 else 1
    steps = (tp // _row_block(tp, _FFN_TM)) * (w1_down.shape[0] // _FFN_TF)
    nb1 = min(bs, steps // (n_pages // gp))
    sb_args = (proj_s4, cache_sb_k.reshape(n_pool, page * sb_heads, sb_dh),
               cache_sb_v.reshape(n_pool, page * sb_heads, sb_dh), page_table, bias_cols, ut2, sb_g)

    memkv = _norm_proj(mem_prompt.reshape(mem_tok, d), row(mem_g), w_mem_b, tn=mix_w)
    if nb1:
        xp, o_sb_s = _ffn_sb(x_prompt.reshape(tp, d), *ffn1, *sb_args, heads=sb_heads, b0=0,
                             n_batches=nb1, pages_per_step=gp)
    else:
        xp, o_sb_s = _ffn(x_prompt.reshape(tp, d), *ffn1), None
    o_sb_parts = [] if o_sb_s is None else [o_sb_s]
    nb2 = min(bs - nb1, steps // (n_pages // gp))
    if nb1 + nb2 < bs:
        o_sb_parts.append(_sb_sample(*sb_args, heads=sb_heads, b0=nb1 + nb2,
                                     n_batches=bs - nb1 - nb2))
    proj_p, qkv_p = _norm_proj(xp, row(mix_g_pre), w_in_b, tn=mix_w, n_bf16=3,
                               bf16_scale0=sb_dh ** -0.5)
    u_strict2 = _strict_upper_pair(V7X_MXU_DIM)
    u_incl = u_strict2[:V7X_MXU_DIM] + jnp.eye(V7X_MXU_DIM, dtype=BF16)
    o_sb_p = _sb_prompt(qkv_p, sb_bias.astype(F32), u_strict2, u_incl, sb_g, heads=sb_heads)
    o_rg_p, h_p = _rglru_prompt(proj_p, *rg_vecs, x_part=P_X, gate_part=P_GR)
    o_mem_p = _mem_attn_prompt(proj_p, memkv, q_part=P_QM, heads=mem_heads)
    xp2 = mix(xp, (o_sb_p, o_rg_p, o_mem_p), proj_p)
    if nb2:
        y_p, part = _ffn_sb(xp2, *ffn2, *sb_args, heads=sb_heads, b0=nb1, n_batches=nb2,
                            pages_per_step=gp)
        o_sb_parts.insert(1 if nb1 else 0, part)
    else:
        y_p = _ffn(xp2, *ffn2)
    o_sb_s = o_sb_parts[0] if len(o_sb_parts) == 1 else jnp.concatenate(o_sb_parts, axis=0)

    to_tm = lambda a: a.reshape(bs, ts, mix_w).transpose(1, 0, 2)
    o_rg_tm, h_s = _rglru_sample(to_tm(proj_s[P_X]), state_conv.transpose(1, 0, 2),
                                 to_tm(proj_s[P_GR]), state_rglru, *rg_vecs, first_pos=past_len)
    o_rg_s = o_rg_tm.transpose(1, 0, 2)
    o_mem_s = _mem_attn_sample(proj_s4, cache_mem_k.reshape(bs, mem_tok, mix_w),
                               cache_mem_v.reshape(bs, mem_tok, mix_w), q_part=P_QM,
                               heads=mem_heads)
    branches_s = tuple(o.reshape(ms, mix_w) for o in (o_sb_s, o_rg_s, o_mem_s))
    y_s = _ffn(mix(xs, branches_s, proj_s), *ffn2)

    n_keep = conv_w.shape[0] - 1
    xr_s = proj_s[P_X].reshape(bs, ts, mix_w)
    conv_s = jnp.concatenate([state_conv.astype(F32), xr_s], axis=1)[:, ts:]
    conv_p = lax.slice(proj_p, (P_X, tp - n_keep, 0), (P_X + 1, tp, mix_w))
    heads4 = lambda a, b_, t_: a.reshape(b_, t_, sb_heads, sb_dh)
    return (
        y_p.reshape(bp, tp, d),
        y_s.reshape(bs, ts, d),
        heads4(proj_p[P_K], bp, tp), heads4(proj_p[P_V], bp, tp),
        heads4(proj_s[P_K], bs, ts), heads4(proj_s[P_V], bs, ts),
        conv_p, conv_s,
        h_p, h_s,
        memkv[0].reshape(bp, mem_tok, mem_heads, mix_w // mem_heads),
        memkv[1].reshape(bp, mem_tok, mem_heads, mix_w // mem_heads),
    )
```
